```python
import jax, jax.numpy as jnp
from jax import lax
import numpy as np

D_MODEL = 2048
BATCH = 4
SEQ = 4096
DEPTH = 2

D_ATTN = D_MODEL // 2
ATTN_HEAD_DIM = 64
N_Q_HEADS = D_ATTN // ATTN_HEAD_DIM
N_KV_HEADS = 4
WINDOW = 128
BLOCK = 128
ROPE_DIM = ATTN_HEAD_DIM // 4
ROPE_THETA = 500000.0

D_RET = D_MODEL - D_ATTN
RET_HEADS = 4
RET_HEAD_DIM = D_RET // RET_HEADS
RET_CHUNK = 128
RET_THETA = 10000.0

D_KV = N_KV_HEADS * ATTN_HEAD_DIM
D_IN = D_ATTN + 2 * D_KV + 4 * D_RET
D_MIX = D_ATTN + D_RET

D_FF = 5632
CONV_W = 3
EPS = 1e-6
NEG_INF = -1e30

kernel_name = "hymba_swa_sink_retention_convffn"


def _rms_norm(x, w):
    xf = x.astype(jnp.float32)
    y = xf * lax.rsqrt(jnp.mean(xf * xf, axis=-1, keepdims=True) + EPS)
    return (y * w.astype(jnp.float32)).astype(x.dtype)


def _rope(x, positions, rot_dim, theta):
    half = rot_dim // 2
    inv = jnp.power(jnp.float32(theta), -jnp.arange(half, dtype=jnp.float32) / half)
    ang = positions.astype(jnp.float32)[..., None] * inv
    cos = jnp.cos(ang)[:, :, None, :].astype(x.dtype)
    sin = jnp.sin(ang)[:, :, None, :].astype(x.dtype)
    x1 = x[..., :half]
    x2 = x[..., half:rot_dim]
    return jnp.concatenate([x1 * cos - x2 * sin, x2 * cos + x1 * sin, x[..., rot_dim:]], axis=-1)


def _swa_sink_attention(q, k, v, sinks):
    B, S, Hq, dh = q.shape
    Hkv = k.shape[2]
    G = Hq // Hkv
    nb = S // BLOCK
    qb = q.reshape(B, nb, BLOCK, Hkv, G, dh)

    def band(t):
        tp = jnp.pad(t, ((0, 0), (BLOCK, 0), (0, 0), (0, 0)))
        tb = tp.reshape(B, nb + 1, BLOCK, Hkv, dh)
        return jnp.concatenate([tb[:, :-1], tb[:, 1:]], axis=2)

    kb = band(k)
    vb = band(v)
    s = jnp.einsum('bnqhgd,bnkhd->bhgnqk', qb, kb).astype(jnp.float32) * (dh ** -0.5)
    qi = jnp.arange(BLOCK)[:, None]
    kj = jnp.arange(2 * BLOCK)[None, :]
    diff = BLOCK + qi - kj
    key_pos = jnp.arange(nb)[:, None, None] * BLOCK + kj[None] - BLOCK
    mask = ((diff >= 0) & (diff < WINDOW))[None] & (key_pos >= 0)
    s = jnp.where(mask, s, NEG_INF)
    sink = sinks.astype(jnp.float32).reshape(Hkv, G)[None, :, :, None, None, None]
    m = jnp.maximum(jnp.max(s, axis=-1, keepdims=True), sink)
    p = jnp.exp(s - m)
    denom = jnp.sum(p, axis=-1, keepdims=True) + jnp.exp(sink - m)
    p = (p / denom).astype(v.dtype)
    o = jnp.einsum('bhgnqk,bnkhd->bnqhgd', p, vb)
    return o.reshape(B, S, Hq * dh)


def _retention(q, k, v):
    B, S, H, d = q.shape
    dv = v.shape[-1]
    nc = S // RET_CHUNK
    q = q.astype(jnp.float32)
    k = k.astype(jnp.float32) * (d ** -0.5)
    v = v.astype(jnp.float32)
    lg = jnp.log(1.0 - jnp.power(2.0, -5.0 - jnp.arange(H, dtype=jnp.float32)))
    idx = jnp.arange(RET_CHUNK, dtype=jnp.float32)
    rel = idx[:, None] - idx[None, :]
    intra = jnp.where(rel >= 0, jnp.exp(lg[:, None, None] * jnp.maximum(rel, 0.0)), 0.0)
    q_dec = jnp.exp(lg[:, None] * (idx + 1.0))[..., None]
    k_dec = jnp.exp(lg[:, None] * (RET_CHUNK - 1.0 - idx))[..., None]
    chunk_dec = jnp.exp(lg * RET_CHUNK)[:, None, None]

    def to_chunks(t):
        return t.reshape(B, nc, RET_CHUNK, H, t.shape[-1]).transpose(1, 0, 3, 2, 4)

    def step(state, xs):
        qc, kc, vc = xs
        inner = jnp.einsum('bhij,bhjd->bhid', jnp.einsum('bhid,bhjd->bhij', qc, kc) * intra, vc)
        cross = jnp.einsum('bhid,bhde->bhie', qc * q_dec, state)
        state = state * chunk_dec + jnp.einsum('bhjd,bhje->bhde', kc * k_dec, vc)
        return state, inner + cross

    state0 = jnp.zeros((B, H, d, dv), jnp.float32)
    _, o = lax.scan(step, state0, (to_chunks(q), to_chunks(k), to_chunks(v)))
    return o.transpose(1, 0, 3, 2, 4).reshape(B, S, H, dv)


def _causal_dwconv(u, w, b):
    S = u.shape[1]
    up = jnp.pad(u, ((0, 0), (CONV_W - 1, 0), (0, 0)))
    y = b.astype(u.dtype)
    for kk in range(CONV_W):
        y = y + up[:, kk:kk + S] * w[kk]
    return y


def setup_inputs(seed: int = 0) -> dict:
    key = jax.random.key(seed)
    ks = jax.random.split(key, 16)
    f32 = jnp.float32
    x = jax.random.normal(ks[0], (BATCH, SEQ, D_MODEL), f32)
    positions = jnp.broadcast_to(jnp.arange(SEQ, dtype=jnp.int32), (BATCH, SEQ))
    w_in = jax.random.normal(ks[1], (DEPTH, D_MODEL, D_IN), f32) * D_MODEL ** -0.5
    w_out = jax.random.normal(ks[2], (DEPTH, D_MIX, D_MODEL), f32) * D_MIX ** -0.5
    w_up = jax.random.normal(ks[3], (DEPTH, D_MODEL, 2 * D_FF), f32) * D_MODEL ** -0.5
    w_down = jax.random.normal(ks[4], (DEPTH, D_FF, D_MODEL), f32) * D_FF ** -0.5
    conv_w = jax.random.normal(ks[5], (DEPTH, CONV_W, 2 * D_FF), f32) * CONV_W ** -0.5
    conv_b = jax.random.normal(ks[6], (DEPTH, 2 * D_FF), f32) * 0.02
    attn_sinks = jax.random.normal(ks[7], (DEPTH, N_Q_HEADS), f32) * 0.5
    def gain(k, shape):
        return 1.0 + 0.02 * jax.random.normal(k, shape, f32)
    pre_mix_norm = gain(ks[8], (DEPTH, D_MODEL))
    post_mix_norm = gain(ks[9], (DEPTH, D_MODEL))
    attn_out_norm = gain(ks[10], (DEPTH, D_ATTN))
    ret_out_norm = gain(ks[11], (DEPTH, RET_HEADS, RET_HEAD_DIM))
    pre_ffn_norm = gain(ks[12], (DEPTH, D_MODEL))
    post_ffn_norm = gain(ks[13], (DEPTH, D_MODEL))
    return {"x": x, "positions": positions, "w_in": w_in, "w_out": w_out,
            "w_up": w_up, "w_down": w_down, "conv_w": conv_w, "conv_b": conv_b,
            "attn_sinks": attn_sinks, "pre_mix_norm": pre_mix_norm,
            "post_mix_norm": post_mix_norm, "attn_out_norm": attn_out_norm,
            "ret_out_norm": ret_out_norm, "pre_ffn_norm": pre_ffn_norm,
            "post_ffn_norm": post_ffn_norm}


def reference(x, positions, w_in, w_out, w_up, w_down, conv_w, conv_b, attn_sinks,
              pre_mix_norm, post_mix_norm, attn_out_norm, ret_out_norm,
              pre_ffn_norm, post_ffn_norm):
    B, S, _ = x.shape
    split_at = [D_ATTN, D_ATTN + D_KV, D_ATTN + 2 * D_KV,
                D_ATTN + 2 * D_KV + D_RET, D_ATTN + 2 * D_KV + 2 * D_RET,
                D_ATTN + 2 * D_KV + 3 * D_RET]
    for l in range(DEPTH):
        h = _rms_norm(x, pre_mix_norm[l])
        proj = h @ w_in[l]
        qa, ka, va, qr, kr, vr, gr = jnp.split(proj, split_at, axis=-1)
        qa = _rope(qa.reshape(B, S, N_Q_HEADS, ATTN_HEAD_DIM), positions, ROPE_DIM, ROPE_THETA)
        ka = _rope(ka.reshape(B, S, N_KV_HEADS, ATTN_HEAD_DIM), positions, ROPE_DIM, ROPE_THETA)
        va = va.reshape(B, S, N_KV_HEADS, ATTN_HEAD_DIM)
        ya = _swa_sink_attention(qa, ka, va, attn_sinks[l])
        ya = _rms_norm(ya, attn_out_norm[l])
        qr = _rope(qr.reshape(B, S, RET_HEADS, RET_HEAD_DIM), positions, RET_HEAD_DIM, RET_THETA)
        kr = _rope(kr.reshape(B, S, RET_HEADS, RET_HEAD_DIM), positions, RET_HEAD_DIM, RET_THETA)
        vr = vr.reshape(B, S, RET_HEADS, RET_HEAD_DIM)
        yr = _rms_norm(_retention(qr, kr, vr), ret_out_norm[l]).astype(x.dtype)
        yr = yr.reshape(B, S, D_RET) * jax.nn.silu(gr)
        mix = jnp.concatenate([ya, yr], axis=-1) @ w_out[l]
        x = x + _rms_norm(mix, post_mix_norm[l])
        h = _rms_norm(x, pre_ffn_norm[l])
        u = _causal_dwconv(h @ w_up[l], conv_w[l], conv_b[l])
        a, g = jnp.split(u, 2, axis=-1)
        f = (jax.nn.gelu(a, approximate=True) * g) @ w_down[l]
        x = x + _rms_norm(f, post_ffn_norm[l])
    return x
```

```python
import functools

import jax
import jax.numpy as jnp
from jax import lax
from jax.experimental import pallas as pl
from jax.experimental.pallas import tpu as pltpu

F32 = jnp.float32
BF16 = jnp.bfloat16

D_MODEL = 2048
D_ATTN = 1024
HEAD_DIM = 64
N_Q_HEADS = 16
N_KV_HEADS = 4
Q_PER_KV = N_Q_HEADS // N_KV_HEADS
BLOCK = 128
ROPE_DIM = 16
ROPE_THETA = 500000.0
D_RET = 1024
RET_HEADS = 4
RET_HEAD_DIM = 256
RET_THETA = 10000.0
D_KV = N_KV_HEADS * HEAD_DIM
D_IN = D_ATTN + 2 * D_KV + 4 * D_RET
D_FF = 5632
CONV_W = 3
EPS = 1e-6
NEG_INF = -1e30

LANES = 128
SUBLANES = 8
VMEM_LIMIT = 56 * 1024 * 1024

TM_IN = 1024
TN_IN = 512
TM_OUT = 256
NORM_ROWS = 128


def _rms_rows(xf, gain):
    ms = jnp.mean(xf * xf, axis=-1, keepdims=True)
    return xf * lax.rsqrt(ms + EPS) * gain


def _norm_into(x_ref, g_ref, h_ref):
    gain = g_ref[...]

    def body(r, carry):
        rows = pl.ds(pl.multiple_of(r * NORM_ROWS, NORM_ROWS), NORM_ROWS)
        h_ref[rows, :] = _rms_rows(x_ref[rows, :], gain).astype(h_ref.dtype)
        return carry

    lax.fori_loop(0, x_ref.shape[0] // NORM_ROWS, body, 0)


def _rope_table_kernel(pos_ref, inv_r_ref, inv_a_ref, rot_ref, sa_ref, sb_ref,
                       rc_ref, rs_ref, ac_ref, asa_ref, asb_ref):
    pos = pos_ref[...]
    ang_r = pos * inv_r_ref[...]
    rc_ref[...] = jnp.cos(ang_r)
    rs_ref[...] = jnp.sin(ang_r)
    ang_a = pos * inv_a_ref[...]
    ca = jnp.cos(ang_a)
    sa = jnp.sin(ang_a)
    ac_ref[...] = jnp.where(rot_ref[...] > 0.5, ca, 1.0)
    asa_ref[...] = sa * sa_ref[...]
    asb_ref[...] = sa * sb_ref[...]


def _rope_tables(pos_b):
    m = pos_b.shape[0]
    half_r = RET_HEAD_DIM // 2
    inv_r = jnp.power(F32(RET_THETA), -jnp.arange(half_r, dtype=F32) / half_r)
    half_a = ROPE_DIM // 2
    inv_a8 = jnp.power(F32(ROPE_THETA), -jnp.arange(half_a, dtype=F32) / half_a)
    lane = jnp.arange(LANES) % HEAD_DIM
    inv_a = jnp.where(lane < ROPE_DIM, inv_a8[lane % half_a], 0.0).astype(F32)
    rot = (lane < ROPE_DIM).astype(F32)
    sa = jnp.where(lane < half_a, -1.0, 0.0).astype(F32)
    sb = jnp.where((lane >= half_a) & (lane < ROPE_DIM), 1.0, 0.0).astype(F32)
    rows = 2048
    row_spec = pl.BlockSpec((rows, LANES), lambda i: (i, 0))
    lane_spec = pl.BlockSpec((1, LANES), lambda i: (0, 0))
    return pl.pallas_call(
        _rope_table_kernel,
        grid=(m // rows,),
        in_specs=[row_spec] + [lane_spec] * 5,
        out_specs=[row_spec] * 5,
        out_shape=[jax.ShapeDtypeStruct((m, LANES), F32)] * 5,
        compiler_params=pltpu.CompilerParams(
            dimension_semantics=("parallel",), vmem_limit_bytes=VMEM_LIMIT),
        name="rope_tables",
    )(pos_b, inv_r[None], inv_a[None], rot[None], sa[None], sb[None])


def _norm_matmul_kernel(x_ref, g_ref, w_ref, o_ref, h_ref):
    @pl.when(pl.program_id(1) == 0)
    def _():
        _norm_into(x_ref, g_ref, h_ref)

    o_ref[...] = jnp.dot(h_ref[...], w_ref[...],
                         preferred_element_type=F32).astype(o_ref.dtype)


def _norm_matmul(x, gain, w):
    m, d = x.shape
    n = w.shape[1]
    return pl.pallas_call(
        _norm_matmul_kernel,
        grid=(m // TM_IN, n // TN_IN),
        in_specs=[pl.BlockSpec((TM_IN, d), lambda i, j: (i, 0)),
                  pl.BlockSpec((1, d), lambda i, j: (0, 0)),
                  pl.BlockSpec((d, TN_IN), lambda i, j: (0, j))],
        out_specs=pl.BlockSpec((TM_IN, TN_IN), lambda i, j: (i, j)),
        out_shape=jax.ShapeDtypeStruct((m, n), BF16),
        scratch_shapes=[pltpu.VMEM((TM_IN, d), BF16)],
        compiler_params=pltpu.CompilerParams(
            dimension_semantics=("parallel", "arbitrary"),
            vmem_limit_bytes=VMEM_LIMIT),
        name="norm_in_proj",
    )(x, gain, w)


def _mixer_kernel(qa_ref, kva_ref, qr_ref, kr_ref, vr_ref, gr_ref,
                  ac_ref, asa_ref, asb_ref, rc_ref, rs_ref,
                  sink_ref, cdec_ref, anorm_ref, rnorm_ref,
                  intra_ref, qdec_ref, kdec_ref,
                  o_ref, kprev_ref, vprev_ref, state_ref, ya_ref):
    n = pl.program_id(1)

    @pl.when(n == 0)
    def _():
        kprev_ref[...] = jnp.zeros_like(kprev_ref)
        vprev_ref[...] = jnp.zeros_like(vprev_ref)
        state_ref[...] = jnp.zeros_like(state_ref)

    ac = ac_ref[...]
    asa = asa_ref[...]
    asb = asb_ref[...]

    def rope_a(xc):
        return (xc * ac + pltpu.roll(xc, LANES - ROPE_DIM // 2, 1) * asa
                + pltpu.roll(xc, ROPE_DIM // 2, 1) * asb)

    scale = HEAD_DIM ** -0.5
    kv = kva_ref[...]
    k_cur = jnp.concatenate(
        [rope_a(kv[:, c * LANES:(c + 1) * LANES].astype(F32)) for c in range(D_KV // LANES)],
        axis=1).astype(BF16)
    v_cur = kv[:, D_KV:]
    k_prev = kprev_ref[...].astype(BF16)
    v_prev = vprev_ref[...].astype(BF16)

    qi = lax.broadcasted_iota(jnp.int32, (BLOCK, BLOCK), 0)
    kj = lax.broadcasted_iota(jnp.int32, (BLOCK, BLOCK), 1)
    first_query = jnp.where(n > 0, 0, BLOCK)
    valid_prev = (kj > qi) & (qi >= first_query)
    valid_cur = kj <= qi
    qk_dims = (((1,), (1,)), ((), ()))

    for c in range(D_ATTN // LANES):
        qc = (rope_a(qa_ref[:, c * LANES:(c + 1) * LANES].astype(F32)) * scale).astype(BF16)
        for hh in range(LANES // HEAD_DIM):
            h = c * (LANES // HEAD_DIM) + hh
            g = h // Q_PER_KV
            head = slice(g * HEAD_DIM, (g + 1) * HEAD_DIM)
            qh = qc[:, hh * HEAD_DIM:(hh + 1) * HEAD_DIM]
            s_prev = lax.dot_general(qh, k_prev[:, head], qk_dims, preferred_element_type=F32)
            s_cur = lax.dot_general(qh, k_cur[:, head], qk_dims, preferred_element_type=F32)
            s_prev = jnp.where(valid_prev, s_prev, NEG_INF)
            s_cur = jnp.where(valid_cur, s_cur, NEG_INF)
            sink = sink_ref[h]
            mx = jnp.maximum(jnp.maximum(jnp.max(s_prev, axis=-1, keepdims=True),
                                         jnp.max(s_cur, axis=-1, keepdims=True)), sink)
            p_prev = jnp.exp(s_prev - mx)
            p_cur = jnp.exp(s_cur - mx)
            denom = (jnp.sum(p_prev, axis=-1, keepdims=True)
                     + jnp.sum(p_cur, axis=-1, keepdims=True) + jnp.exp(sink - mx))
            oh = (jnp.dot(p_prev.astype(BF16), v_prev[:, head], preferred_element_type=F32)
                  + jnp.dot(p_cur.astype(BF16), v_cur[:, head], preferred_element_type=F32))
            ya_ref[:, h * HEAD_DIM:(h + 1) * HEAD_DIM] = oh / denom

    kprev_ref[...] = k_cur.astype(F32)
    vprev_ref[...] = v_cur.astype(F32)

    o_ref[:, :D_ATTN] = _rms_rows(ya_ref[...], anorm_ref[...]).astype(o_ref.dtype)

    rc = rc_ref[...]
    rs = rs_ref[...]
    half = RET_HEAD_DIM // 2

    def rope_r(x):
        x1 = x[:, :half]
        x2 = x[:, half:]
        return x1 * rc - x2 * rs, x2 * rc + x1 * rs

    kscale = RET_HEAD_DIM ** -0.5
    for h in range(RET_HEADS):
        cols = slice(h * RET_HEAD_DIM, (h + 1) * RET_HEAD_DIM)
        q1, q2 = rope_r(qr_ref[:, cols].astype(F32))
        k1, k2 = rope_r(kr_ref[:, cols].astype(F32))
        k1 = k1 * kscale
        k2 = k2 * kscale
        vh = vr_ref[:, cols]
        qd = qdec_ref[h]
        kd = kdec_ref[h]
        q_b = jnp.concatenate([q1, q2], axis=1).astype(BF16)
        k_b = jnp.concatenate([k1, k2], axis=1).astype(BF16)
        qd_b = jnp.concatenate([q1 * qd, q2 * qd], axis=1).astype(BF16)
        kd_b = jnp.concatenate([k1 * kd, k2 * kd], axis=1).astype(BF16)
        sc = lax.dot_general(q_b, k_b, (((1,), (1,)), ((), ())),
                             preferred_element_type=F32)
        inner = jnp.dot((sc * intra_ref[h]).astype(BF16), vh, preferred_element_type=F32)
        state = state_ref[h]
        cross = jnp.dot(qd_b, state.astype(BF16), preferred_element_type=F32)
        state_ref[h] = state * cdec_ref[h] + lax.dot_general(
            kd_b, vh, (((0,), (0,)), ((), ())), preferred_element_type=F32)
        y = _rms_rows(inner + cross, rnorm_ref[:, cols])
        gate = gr_ref[:, cols].astype(F32)
        o_ref[:, D_ATTN + h * RET_HEAD_DIM:D_ATTN + (h + 1) * RET_HEAD_DIM] = (
            y * (gate * jax.nn.sigmoid(gate))).astype(o_ref.dtype)


def _mixer(proj, tables, sinks, cdec, anorm, rnorm, intra, qdec, kdec, batch):
    m = proj.shape[0]
    nb = m // batch // BLOCK
    rc, rs, ac, asa, asb = tables

    def rows(width, col):
        return pl.BlockSpec((BLOCK, width), lambda b, n: (b * nb + n, col))

    tab_spec = rows(LANES, 0)
    smem = pl.BlockSpec(memory_space=pltpu.SMEM)

    def const(shape):
        return pl.BlockSpec(shape, lambda b, n: (0,) * len(shape))

    dec_shape = (RET_HEADS, BLOCK, LANES)
    return pl.pallas_call(
        _mixer_kernel,
        grid=(batch, nb),
        in_specs=[rows(D_ATTN, 0),
                  rows(2 * D_KV, 10),
                  rows(D_RET, 1), rows(D_RET, 2), rows(D_RET, 3), rows(D_RET, 4),
                  tab_spec, tab_spec, tab_spec, tab_spec, tab_spec,
                  smem, smem,
                  const((1, D_ATTN)), const((1, D_RET)),
                  const((RET_HEADS, BLOCK, BLOCK)), const(dec_shape), const(dec_shape)],
        out_specs=rows(D_ATTN + D_RET, 0),
        out_shape=jax.ShapeDtypeStruct((m, D_ATTN + D_RET), BF16),
        scratch_shapes=[pltpu.VMEM((BLOCK, D_KV), F32),
                        pltpu.VMEM((BLOCK, D_KV), F32),
                        pltpu.VMEM((RET_HEADS, RET_HEAD_DIM, RET_HEAD_DIM), F32),
                        pltpu.VMEM((BLOCK, D_ATTN), F32)],
        compiler_params=pltpu.CompilerParams(
            dimension_semantics=("parallel", "arbitrary"),
            vmem_limit_bytes=VMEM_LIMIT),
        name="token_mixer",
    )(proj, proj, proj, proj, proj, proj, ac, asa, asb, rc, rs,
      sinks, cdec, anorm, rnorm, intra, qdec, kdec)


def _matmul_norm_res_kernel(a_ref, w_ref, x_ref, g_ref, o_ref):
    y = jnp.dot(a_ref[...], w_ref[...], preferred_element_type=F32)
    o_ref[...] = x_ref[...] + _rms_rows(y, g_ref[...])


def _matmul_norm_res(a, w, x, gain):
    m, k = a.shape
    d = w.shape[1]
    return pl.pallas_call(
        _matmul_norm_res_kernel,
        grid=(m // TM_OUT,),
        in_specs=[pl.BlockSpec((TM_OUT, k), lambda i: (i, 0)),
                  pl.BlockSpec((k, d), lambda i: (0, 0), pipeline_mode=pl.Buffered(1)),
                  pl.BlockSpec((TM_OUT, d), lambda i: (i, 0)),
                  pl.BlockSpec((1, d), lambda i: (0, 0))],
        out_specs=pl.BlockSpec((TM_OUT, d), lambda i: (i, 0)),
        out_shape=jax.ShapeDtypeStruct((m, d), F32),
        compiler_params=pltpu.CompilerParams(
            dimension_semantics=("parallel",), vmem_limit_bytes=VMEM_LIMIT),
        name="proj_norm_residual",
    )(a, w, x, gain)


def _ffn_up_kernel(tiles_per_seq, x_ref, g_ref, wa_ref, wg_ref, cwa_ref, cwg_ref,
                   cba_ref, cbg_ref, o_ref, h_ref, ubuf_ref, carry_ref):
    i = pl.program_id(0)
    j = pl.program_id(1)
    tm = x_ref.shape[0]

    @pl.when(j == 0)
    def _():
        _norm_into(x_ref, g_ref, h_ref)

    @pl.when((i == 0) & (j == 0))
    def _():
        carry_ref[...] = jnp.zeros_like(carry_ref)

    starts_sequence = (i % tiles_per_seq) == 0
    h = h_ref[...]

    def conv_branch(slot, w_ref, cw_ref, cb_ref):
        u = jnp.dot(h, w_ref[...], preferred_element_type=F32)
        prev = carry_ref[slot, j]
        ubuf_ref[0:SUBLANES, :] = jnp.where(starts_sequence, 0.0, prev)
        ubuf_ref[SUBLANES:SUBLANES + tm, :] = u
        carry_ref[slot, j] = u[tm - SUBLANES:, :]
        cw = cw_ref[...]
        return (cb_ref[...] + ubuf_ref[SUBLANES - 2:SUBLANES - 2 + tm, :] * cw[0:1]
                + ubuf_ref[SUBLANES - 1:SUBLANES - 1 + tm, :] * cw[1:2] + u * cw[2:3])

    a = conv_branch(0, wa_ref, cwa_ref, cba_ref)
    gate = conv_branch(1, wg_ref, cwg_ref, cbg_ref)
    o_ref[...] = (jax.nn.gelu(a, approximate=True) * gate).astype(o_ref.dtype)


def _ffn_up(x, gain, w_up, conv_w, conv_b, seq):
    m, d = x.shape
    nj = D_FF // TN_IN
    kern = functools.partial(_ffn_up_kernel, seq // TM_IN)
    return pl.pallas_call(
        kern,
        grid=(m // TM_IN, nj),
        in_specs=[pl.BlockSpec((TM_IN, d), lambda i, j: (i, 0)),
                  pl.BlockSpec((1, d), lambda i, j: (0, 0)),
                  pl.BlockSpec((d, TN_IN), lambda i, j: (0, j)),
                  pl.BlockSpec((d, TN_IN), lambda i, j: (0, j + nj)),
                  pl.BlockSpec((CONV_W, TN_IN), lambda i, j: (0, j)),
                  pl.BlockSpec((CONV_W, TN_IN), lambda i, j: (0, j + nj)),
                  pl.BlockSpec((1, TN_IN), lambda i, j: (0, j)),
                  pl.BlockSpec((1, TN_IN), lambda i, j: (0, j + nj))],
        out_specs=pl.BlockSpec((TM_IN, TN_IN), lambda i, j: (i, j)),
        out_shape=jax.ShapeDtypeStruct((m, D_FF), BF16),
        scratch_shapes=[pltpu.VMEM((TM_IN, d), BF16),
                        pltpu.VMEM((TM_IN + SUBLANES, TN_IN), F32),
                        pltpu.VMEM((2, nj, SUBLANES, TN_IN), F32)],
        compiler_params=pltpu.CompilerParams(
            dimension_semantics=("arbitrary", "arbitrary"),
            vmem_limit_bytes=VMEM_LIMIT),
        name="ffn_up_conv_gate",
    )(x, gain, w_up, w_up, conv_w, conv_w, conv_b, conv_b)


def _retention_decay_tables():
    lg = jnp.log(1.0 - jnp.power(2.0, -5.0 - jnp.arange(RET_HEADS, dtype=F32)))
    idx = jnp.arange(BLOCK, dtype=F32)
    rel = idx[:, None] - idx[None, :]
    intra = jnp.where(rel >= 0, jnp.exp(lg[:, None, None] * jnp.maximum(rel, 0.0)), 0.0)
    q_dec = jnp.exp(lg[:, None] * (idx + 1.0))[..., None]
    k_dec = jnp.exp(lg[:, None] * (BLOCK - 1.0 - idx))[..., None]
    chunk_dec = jnp.exp(lg * BLOCK)
    shape = (RET_HEADS, BLOCK, LANES)
    return (intra.astype(F32), jnp.broadcast_to(q_dec, shape).astype(F32),
            jnp.broadcast_to(k_dec, shape).astype(F32), chunk_dec.astype(F32))


def _reorder_in_proj(w):
    a0, a1, a2 = D_ATTN, D_ATTN + D_KV, D_ATTN + 2 * D_KV
    return jnp.concatenate([w[:, :a0], w[:, a2:], w[:, a0:a1], w[:, a1:a2]], axis=1)


def kernel(x, positions, w_in, w_out, w_up, w_down, conv_w, conv_b, attn_sinks,
           pre_mix_norm, post_mix_norm, attn_out_norm, ret_out_norm,
           pre_ffn_norm, post_ffn_norm):
    batch, seq, d = x.shape
    m = batch * seq
    depth = w_in.shape[0]
    xf = x.reshape(m, d)
    pos_b = jnp.broadcast_to(positions.reshape(m, 1).astype(F32), (m, LANES))
    tables = _rope_tables(pos_b)
    intra, qdec, kdec, cdec = _retention_decay_tables()
    for l in range(depth):
        w_in_l = _reorder_in_proj(w_in[l]).astype(BF16)
        proj = _norm_matmul(xf, pre_mix_norm[l][None], w_in_l)
        mix = _mixer(proj, tables, attn_sinks[l], cdec, attn_out_norm[l][None],
                     ret_out_norm[l].reshape(1, D_RET), intra, qdec, kdec, batch)
        xf = _matmul_norm_res(mix, w_out[l].astype(BF16), xf, post_mix_norm[l][None])
        f = _ffn_up(xf, pre_ffn_norm[l][None], w_up[l].astype(BF16), conv_w[l],
                    conv_b[l][None], seq)
        xf = _matmul_norm_res(f, w_down[l].astype(BF16), xf, post_ffn_norm[l][None])
    return xf.reshape(batch, seq, d)
```

```python
import functools

import jax
import jax.numpy as jnp
from jax import lax
from jax.experimental import pallas as pl
from jax.experimental.pallas import tpu as pltpu

F32 = jnp.float32
BF16 = jnp.bfloat16

D_MODEL = 2048
D_ATTN = 1024
HEAD_DIM = 64
N_Q_HEADS = 16
N_KV_HEADS = 4
Q_PER_KV = N_Q_HEADS // N_KV_HEADS
BLOCK = 128
ROPE_DIM = 16
ROPE_THETA = 500000.0
D_RET = 1024
RET_HEADS = 4
RET_HEAD_DIM = 256
RET_THETA = 10000.0
D_KV = N_KV_HEADS * HEAD_DIM
D_IN = D_ATTN + 2 * D_KV + 4 * D_RET
D_FF = 5632
CONV_W = 3
EPS = 1e-6
NEG_INF = -1e30
LOG2E = 1.4426950408889634

LANES = 128
SUBLANES = 8
VMEM_LIMIT = 56 * 1024 * 1024

TM_IN = 1024
TN_IN = 512
TM_OUT = 256
NORM_ROWS = 128

Q_TILES = D_ATTN // TN_IN
ROPE_R_TILES = (Q_TILES, Q_TILES + 2 * D_RET // TN_IN)
KV_TILE = (D_ATTN + 4 * D_RET) // TN_IN
HEADS_PER_TILE = TN_IN // RET_HEAD_DIM


def _rms_rows(xf, gain):
    ms = jnp.mean(xf * xf, axis=-1, keepdims=True)
    return xf * lax.rsqrt(ms + EPS) * gain


def _norm_into(x_ref, g_ref, h_ref):
    gain = g_ref[...]

    def body(r, carry):
        rows = pl.ds(pl.multiple_of(r * NORM_ROWS, NORM_ROWS), NORM_ROWS)
        h_ref[rows, :] = _rms_rows(x_ref[rows, :], gain).astype(h_ref.dtype)
        return carry

    lax.fori_loop(0, x_ref.shape[0] // NORM_ROWS, body, 0)


def _rope_table_kernel(pos_ref, inv_r_ref, inv_a_ref, rot_ref, sa_ref, sb_ref,
                       rc_ref, rs_ref, ac_ref, asa_ref, asb_ref):
    pos = pos_ref[...]
    ang_r = pos * inv_r_ref[...]
    rc_ref[...] = jnp.cos(ang_r)
    rs_ref[...] = jnp.sin(ang_r)
    ang_a = pos * inv_a_ref[...]
    ca = jnp.cos(ang_a)
    sa = jnp.sin(ang_a)
    ac_ref[...] = jnp.where(rot_ref[...] > 0.5, ca, 1.0)
    asa_ref[...] = sa * sa_ref[...]
    asb_ref[...] = sa * sb_ref[...]


def _rope_tables(pos_b):
    m = pos_b.shape[0]
    half_r = RET_HEAD_DIM // 2
    inv_r = jnp.power(F32(RET_THETA), -jnp.arange(half_r, dtype=F32) / half_r)
    half_a = ROPE_DIM // 2
    inv_a8 = jnp.power(F32(ROPE_THETA), -jnp.arange(half_a, dtype=F32) / half_a)
    lane = jnp.arange(LANES) % HEAD_DIM
    inv_a = jnp.where(lane < ROPE_DIM, inv_a8[lane % half_a], 0.0).astype(F32)
    rot = (lane < ROPE_DIM).astype(F32)
    sa = jnp.where(lane < half_a, -1.0, 0.0).astype(F32)
    sb = jnp.where((lane >= half_a) & (lane < ROPE_DIM), 1.0, 0.0).astype(F32)
    rows = 2048
    row_spec = pl.BlockSpec((rows, LANES), lambda i: (i, 0))
    lane_spec = pl.BlockSpec((1, LANES), lambda i: (0, 0))
    return pl.pallas_call(
        _rope_table_kernel,
        grid=(m // rows,),
        in_specs=[row_spec] + [lane_spec] * 5,
        out_specs=[row_spec] * 5,
        out_shape=[jax.ShapeDtypeStruct((m, LANES), F32)] * 5,
        compiler_params=pltpu.CompilerParams(
            dimension_semantics=("parallel",), vmem_limit_bytes=VMEM_LIMIT),
        name="rope_tables",
    )(pos_b, inv_r[None], inv_a[None], rot[None], sa[None], sb[None])


def _in_proj_kernel(x_ref, g_ref, w_ref, ac_ref, asa_ref, asb_ref, rc_ref, rs_ref, dec_ref,
                    o_ref, h_ref):
    j = pl.program_id(1)
    tm = x_ref.shape[0]

    @pl.when(j == 0)
    def _():
        _norm_into(x_ref, g_ref, h_ref)

    acc = jnp.dot(h_ref[...], w_ref[...], preferred_element_type=F32)

    def attn_rope(n_cols, scale):
        for r in range(tm // BLOCK):
            rows = slice(r * BLOCK, (r + 1) * BLOCK)
            cos, sin_a, sin_b = ac_ref[rows, :], asa_ref[rows, :], asb_ref[rows, :]
            for c in range(n_cols):
                cols = slice(c * LANES, (c + 1) * LANES)
                xc = acc[rows, cols]
                y = (xc * cos + pltpu.roll(xc, LANES - ROPE_DIM // 2, 1) * sin_a
                     + pltpu.roll(xc, ROPE_DIM // 2, 1) * sin_b)
                if scale is not None:
                    y = y * scale
                o_ref[rows, cols] = y.astype(o_ref.dtype)

    @pl.when(j < Q_TILES)
    def _():
        attn_rope(TN_IN // LANES, HEAD_DIM ** -0.5 * LOG2E)

    @pl.when(j == KV_TILE)
    def _():
        attn_rope(D_KV // LANES, None)
        o_ref[:, D_KV:] = acc[:, D_KV:].astype(o_ref.dtype)

    @pl.when((j >= ROPE_R_TILES[0]) & (j < ROPE_R_TILES[1]))
    def _():
        half = RET_HEAD_DIM // 2
        first = (j - ROPE_R_TILES[0]) * HEADS_PER_TILE
        for hh in range(HEADS_PER_TILE):
            dec = dec_ref[first + hh]
            lo = slice(hh * RET_HEAD_DIM, hh * RET_HEAD_DIM + half)
            hi = slice(hh * RET_HEAD_DIM + half, (hh + 1) * RET_HEAD_DIM)
            for r in range(tm // BLOCK):
                rows = slice(r * BLOCK, (r + 1) * BLOCK)
                cos, sin = rc_ref[rows, :], rs_ref[rows, :]
                x1 = acc[rows, lo]
                x2 = acc[rows, hi]
                o_ref[rows, lo] = ((x1 * cos - x2 * sin) * dec).astype(o_ref.dtype)
                o_ref[rows, hi] = ((x2 * cos + x1 * sin) * dec).astype(o_ref.dtype)

    @pl.when((j >= ROPE_R_TILES[1]) & (j < KV_TILE))
    def _():
        o_ref[...] = acc.astype(o_ref.dtype)


def _in_proj(x, gain, w, layer, tables, dec):
    m, d = x.shape
    n = w.shape[2]
    rc, rs, ac, asa, asb = tables
    tab_spec = pl.BlockSpec((TM_IN, LANES), lambda i, j: (i, 0))
    return pl.pallas_call(
        _in_proj_kernel,
        grid=(m // TM_IN, n // TN_IN),
        in_specs=[pl.BlockSpec((TM_IN, d), lambda i, j: (i, 0)),
                  pl.BlockSpec((None, 1, d), lambda i, j: (layer, 0, 0)),
                  pl.BlockSpec((None, d, TN_IN), lambda i, j: (layer, 0, j)),
                  tab_spec, tab_spec, tab_spec, tab_spec, tab_spec,
                  pl.BlockSpec(dec.shape, lambda i, j: (0, 0, 0))],
        out_specs=pl.BlockSpec((TM_IN, TN_IN), lambda i, j: (i, j)),
        out_shape=jax.ShapeDtypeStruct((m, n), BF16),
        scratch_shapes=[pltpu.VMEM((TM_IN, d), BF16)],
        compiler_params=pltpu.CompilerParams(
            dimension_semantics=("parallel", "arbitrary"),
            vmem_limit_bytes=VMEM_LIMIT),
        name="norm_in_proj_rope",
    )(x, gain, w, ac, asa, asb, rc, rs, dec)


def _mixer_kernel(qa_ref, kc_ref, kp_ref, vc_ref, vp_ref, qr_ref, kr_ref, vr_ref, gr_ref,
                  sink_ref, cdec_ref, anorm_ref, rnorm_ref,
                  o_ref, state_ref, ya_ref):
    n = pl.program_id(1)

    @pl.when(n == 0)
    def _():
        state_ref[...] = jnp.zeros_like(state_ref)

    band = 2 * BLOCK
    lane = lax.broadcasted_iota(jnp.int32, (band, LANES), 1)
    low = lane < HEAD_DIM
    ones_lo = jnp.where(low, 1.0, 0.0).astype(BF16)
    ones_hi = jnp.where(low, 0.0, 1.0).astype(BF16)

    def split_heads(prev_ref, cur_ref, c):
        cols = slice(c * LANES, (c + 1) * LANES)
        x = jnp.concatenate([prev_ref[:, cols], cur_ref[:, cols]], axis=0).astype(F32)
        xr = pltpu.roll(x, HEAD_DIM, 1)
        even = (jnp.where(low, x, 0.0).astype(BF16), jnp.where(low, 0.0, xr).astype(BF16))
        odd = (jnp.where(low, xr, 0.0).astype(BF16), jnp.where(low, 0.0, x).astype(BF16))
        return even, odd

    k_lo, k_hi, v_lo, v_hi = [], [], [], []
    for c in range(D_KV // LANES):
        for (ka, kb), (va, vb) in zip(split_heads(kp_ref, kc_ref, c),
                                      split_heads(vp_ref, vc_ref, c)):
            k_lo.append(ka)
            k_hi.append(kb)
            v_lo.append(jnp.concatenate([va, ones_lo], axis=1))
            v_hi.append(jnp.concatenate([vb, ones_hi], axis=1))

    qi = lax.broadcasted_iota(jnp.int32, (BLOCK, band), 0)
    kj = lax.broadcasted_iota(jnp.int32, (BLOCK, band), 1)
    diff = BLOCK + qi - kj
    first_key = jnp.where(n > 0, 0, BLOCK)
    valid = (diff >= 0) & (diff < BLOCK) & (kj >= first_key)
    low_q = lax.broadcasted_iota(jnp.int32, (BLOCK, LANES), 1) < HEAD_DIM
    qk_dims = (((1,), (1,)), ((), ()))

    for c in range(D_ATTN // LANES):
        g = (2 * c) // Q_PER_KV
        qc = qa_ref[:, c * LANES:(c + 1) * LANES]
        acc = None
        sink_terms = []
        for hh, (kmat, vmat) in enumerate(((k_lo[g], v_lo[g]), (k_hi[g], v_hi[g]))):
            s = lax.dot_general(qc, kmat, qk_dims, preferred_element_type=F32)
            s = jnp.where(valid, s, NEG_INF)
            sink = sink_ref[2 * c + hh] * LOG2E
            mx = jnp.maximum(jnp.max(s, axis=-1, keepdims=True), sink)
            p = jnp.exp2(s - mx).astype(BF16)
            od = jnp.dot(p, vmat, preferred_element_type=F32)
            acc = od if acc is None else acc + od
            sink_terms.append(jnp.exp2(sink - mx))
        denom = acc[:, LANES:] + jnp.where(low_q, sink_terms[0], sink_terms[1])
        ya_ref[:, c * LANES:(c + 1) * LANES] = acc[:, :LANES] / denom

    o_ref[:, :D_ATTN] = _rms_rows(ya_ref[...], anorm_ref[...]).astype(o_ref.dtype)

    ri = lax.broadcasted_iota(jnp.int32, (BLOCK, BLOCK), 0)
    rj = lax.broadcasted_iota(jnp.int32, (BLOCK, BLOCK), 1)
    causal = ri >= rj
    for h in range(RET_HEADS):
        cols = slice(h * RET_HEAD_DIM, (h + 1) * RET_HEAD_DIM)
        q = qr_ref[:, cols]
        k = kr_ref[:, cols]
        v = vr_ref[:, cols]
        sc = lax.dot_general(q, k, qk_dims, preferred_element_type=F32)
        sc = jnp.where(causal, sc, 0.0).astype(BF16)
        state = state_ref[h]
        y = (jnp.dot(sc, v, preferred_element_type=F32)
             + jnp.dot(q, state.astype(BF16), preferred_element_type=F32))
        kv = lax.dot_general(k, v, (((0,), (0,)), ((), ())), preferred_element_type=F32)
        state_ref[h] = (state + kv) * cdec_ref[h]
        y = _rms_rows(y, rnorm_ref[:, cols])
        gate = gr_ref[:, cols].astype(F32)
        o_ref[:, D_ATTN + h * RET_HEAD_DIM:D_ATTN + (h + 1) * RET_HEAD_DIM] = (
            y * (gate * jax.nn.sigmoid(gate))).astype(o_ref.dtype)


def _mixer(proj, sinks, cdec, anorm, rnorm, layer, batch):
    m = proj.shape[0]
    nb = m // batch // BLOCK

    def rows(width, col):
        return pl.BlockSpec((BLOCK, width), lambda b, n: (b * nb + n, col))

    def prev_rows(width, col):
        return pl.BlockSpec((BLOCK, width), lambda b, n: (b * nb + jnp.maximum(n - 1, 0), col))

    k_col = (D_ATTN + 4 * D_RET) // D_KV
    smem = pl.BlockSpec(memory_space=pltpu.SMEM)
    return pl.pallas_call(
        _mixer_kernel,
        grid=(batch, nb),
        in_specs=[rows(D_ATTN, 0),
                  rows(D_KV, k_col), prev_rows(D_KV, k_col),
                  rows(D_KV, k_col + 1), prev_rows(D_KV, k_col + 1),
                  rows(D_RET, 1), rows(D_RET, 2), rows(D_RET, 3), rows(D_RET, 4),
                  smem, smem,
                  pl.BlockSpec((None, 1, D_ATTN), lambda b, n: (layer, 0, 0)),
                  pl.BlockSpec((None, 1, D_RET), lambda b, n: (layer, 0, 0))],
        out_specs=rows(D_ATTN + D_RET, 0),
        out_shape=jax.ShapeDtypeStruct((m, D_ATTN + D_RET), BF16),
        scratch_shapes=[pltpu.VMEM((RET_HEADS, RET_HEAD_DIM, RET_HEAD_DIM), F32),
                        pltpu.VMEM((BLOCK, D_ATTN), F32)],
        compiler_params=pltpu.CompilerParams(
            dimension_semantics=("parallel", "arbitrary"),
            vmem_limit_bytes=VMEM_LIMIT),
        name="token_mixer",
    )(proj, proj, proj, proj, proj, proj, proj, proj, proj, sinks, cdec, anorm, rnorm)


def _matmul_norm_res_kernel(a_ref, w_ref, x_ref, g_ref, o_ref):
    y = jnp.dot(a_ref[...], w_ref[...], preferred_element_type=F32)
    o_ref[...] = x_ref[...] + _rms_rows(y, g_ref[...])


def _matmul_norm_res(a, w, x, gain, layer):
    m, k = a.shape
    d = w.shape[2]
    return pl.pallas_call(
        _matmul_norm_res_kernel,
        grid=(m // TM_OUT,),
        in_specs=[pl.BlockSpec((TM_OUT, k), lambda i: (i, 0)),
                  pl.BlockSpec((None, k, d), lambda i: (layer, 0, 0),
                               pipeline_mode=pl.Buffered(1)),
                  pl.BlockSpec((TM_OUT, d), lambda i: (i, 0)),
                  pl.BlockSpec((None, 1, d), lambda i: (layer, 0, 0))],
        out_specs=pl.BlockSpec((TM_OUT, d), lambda i: (i, 0)),
        out_shape=jax.ShapeDtypeStruct((m, d), F32),
        compiler_params=pltpu.CompilerParams(
            dimension_semantics=("parallel",), vmem_limit_bytes=VMEM_LIMIT),
        name="proj_norm_residual",
    )(a, w, x, gain)


def _ffn_up_kernel(tiles_per_seq, x_ref, g_ref, wa_ref, wg_ref, cwa_ref, cwg_ref,
                   cba_ref, cbg_ref, o_ref, h_ref, ubuf_ref, carry_ref):
    i = pl.program_id(0)
    j = pl.program_id(1)
    tm = x_ref.shape[0]

    @pl.when(j == 0)
    def _():
        _norm_into(x_ref, g_ref, h_ref)

    @pl.when((i == 0) & (j == 0))
    def _():
        carry_ref[...] = jnp.zeros_like(carry_ref)

    starts_sequence = (i % tiles_per_seq) == 0
    h = h_ref[...]

    def conv_branch(slot, w_ref, cw_ref, cb_ref):
        u = jnp.dot(h, w_ref[...], preferred_element_type=F32)
        prev = carry_ref[slot, j]
        ubuf_ref[0:SUBLANES, :] = jnp.where(starts_sequence, 0.0, prev)
        ubuf_ref[SUBLANES:SUBLANES + tm, :] = u
        carry_ref[slot, j] = u[tm - SUBLANES:, :]
        cw = cw_ref[...]
        return (cb_ref[...] + ubuf_ref[SUBLANES - 2:SUBLANES - 2 + tm, :] * cw[0:1]
                + ubuf_ref[SUBLANES - 1:SUBLANES - 1 + tm, :] * cw[1:2] + u * cw[2:3])

    a = conv_branch(0, wa_ref, cwa_ref, cba_ref)
    gate = conv_branch(1, wg_ref, cwg_ref, cbg_ref)
    o_ref[...] = (jax.nn.gelu(a, approximate=True) * gate).astype(o_ref.dtype)


def _ffn_up(x, gain, w_up, conv_w, conv_b, layer, seq):
    m, d = x.shape
    nj = D_FF // TN_IN
    kern = functools.partial(_ffn_up_kernel, seq // TM_IN)
    return pl.pallas_call(
        kern,
        grid=(m // TM_IN, nj),
        in_specs=[pl.BlockSpec((TM_IN, d), lambda i, j: (i, 0)),
                  pl.BlockSpec((None, 1, d), lambda i, j: (layer, 0, 0)),
                  pl.BlockSpec((None, d, TN_IN), lambda i, j: (layer, 0, j)),
                  pl.BlockSpec((None, d, TN_IN), lambda i, j: (layer, 0, j + nj)),
                  pl.BlockSpec((None, CONV_W, TN_IN), lambda i, j: (layer, 0, j)),
                  pl.BlockSpec((None, CONV_W, TN_IN), lambda i, j: (layer, 0, j + nj)),
                  pl.BlockSpec((None, 1, TN_IN), lambda i, j: (layer, 0, j)),
                  pl.BlockSpec((None, 1, TN_IN), lambda i, j: (layer, 0, j + nj))],
        out_specs=pl.BlockSpec((TM_IN, TN_IN), lambda i, j: (i, j)),
        out_shape=jax.ShapeDtypeStruct((m, D_FF), BF16),
        scratch_shapes=[pltpu.VMEM((TM_IN, d), BF16),
                        pltpu.VMEM((TM_IN + SUBLANES, TN_IN), F32),
                        pltpu.VMEM((2, nj, SUBLANES, TN_IN), F32)],
        compiler_params=pltpu.CompilerParams(
            dimension_semantics=("arbitrary", "arbitrary"),
            vmem_limit_bytes=VMEM_LIMIT),
        name="ffn_up_conv_gate",
    )(x, gain, w_up, w_up, conv_w, conv_w, conv_b, conv_b)


def _retention_decay_tables():
    lg = jnp.log(1.0 - jnp.power(2.0, -5.0 - jnp.arange(RET_HEADS, dtype=F32)))
    idx = jnp.arange(BLOCK, dtype=F32)
    q_dec = jnp.exp(lg[:, None] * idx)
    k_dec = jnp.exp(-lg[:, None] * idx) * RET_HEAD_DIM ** -0.5
    dec = jnp.concatenate([q_dec, k_dec], axis=0)[..., None]
    dec = jnp.broadcast_to(dec, (2 * RET_HEADS, BLOCK, LANES)).astype(F32)
    chunk_dec = jnp.exp(lg * BLOCK).astype(F32)
    return dec, chunk_dec


def _reorder_in_proj(w):
    a0, a1, a2 = D_ATTN, D_ATTN + D_KV, D_ATTN + 2 * D_KV
    return jnp.concatenate([w[..., :a0], w[..., a2:], w[..., a0:a1], w[..., a1:a2]], axis=-1)


def kernel(x, positions, w_in, w_out, w_up, w_down, conv_w, conv_b, attn_sinks,
           pre_mix_norm, post_mix_norm, attn_out_norm, ret_out_norm,
           pre_ffn_norm, post_ffn_norm):
    batch, seq, d = x.shape
    m = batch * seq
    depth = w_in.shape[0]
    xf = x.reshape(m, d)
    pos_b = jnp.broadcast_to(positions.reshape(m, 1).astype(F32), (m, LANES))
    tables = _rope_tables(pos_b)
    dec, cdec = _retention_decay_tables()
    w_in_b = _reorder_in_proj(w_in).astype(BF16)
    w_out_b = w_out.astype(BF16)
    w_up_b = w_up.astype(BF16)
    w_down_b = w_down.astype(BF16)
    pre_mix = pre_mix_norm[:, None]
    post_mix = post_mix_norm[:, None]
    pre_ffn = pre_ffn_norm[:, None]
    post_ffn = post_ffn_norm[:, None]
    anorm = attn_out_norm[:, None]
    rnorm = ret_out_norm.reshape(depth, 1, D_RET)
    conv_b3 = conv_b[:, None]
    for l in range(depth):
        proj = _in_proj(xf, pre_mix, w_in_b, l, tables, dec)
        mix = _mixer(proj, attn_sinks[l], cdec, anorm, rnorm, l, batch)
        xf = _matmul_norm_res(mix, w_out_b, xf, post_mix, l)
        f = _ffn_up(xf, pre_ffn, w_up_b, conv_w, conv_b3, l, seq)
        xf = _matmul_norm_res(f, w_down_b, xf, post_ffn, l)
    return xf.reshape(batch, seq, d)
```

```python
import functools

import jax
import jax.numpy as jnp
from jax import lax
from jax.experimental import pallas as pl
from jax.experimental.pallas import tpu as pltpu

F32 = jnp.float32
BF16 = jnp.bfloat16

D_MODEL = 2048
D_ATTN = 1024
HEAD_DIM = 64
N_Q_HEADS = 16
N_KV_HEADS = 4
Q_PER_KV = N_Q_HEADS // N_KV_HEADS
BLOCK = 128
ROPE_DIM = 16
ROPE_THETA = 500000.0
D_RET = 1024
RET_HEADS = 4
RET_HEAD_DIM = 256
RET_THETA = 10000.0
D_KV = N_KV_HEADS * HEAD_DIM
D_IN = D_ATTN + 2 * D_KV + 4 * D_RET
D_FF = 5632
CONV_W = 3
EPS = 1e-6
NEG_INF = -1e30
LOG2E = 1.4426950408889634

LANES = 128
SUBLANES = 8
VMEM_LIMIT = 56 * 1024 * 1024

TM_IN = 1024
TN_IN = 512
TM_OUT = 256
NORM_ROWS = 128
GELU_C1 = 0.7978845608028654
GELU_C2 = GELU_C1 * 0.044715

Q_TILES = D_ATTN // TN_IN
KV_TILE = Q_TILES
RET_TILE0 = (D_ATTN + 2 * D_KV) // TN_IN
ROPE_R_TILES = (RET_TILE0, RET_TILE0 + 2 * D_RET // TN_IN)
HEADS_PER_TILE = TN_IN // RET_HEAD_DIM
assert 2 * D_KV == TN_IN and (D_ATTN + 2 * D_KV) % TN_IN == 0


def _rms_rows(xf, gain):
    ms = jnp.mean(xf * xf, axis=-1, keepdims=True)
    return xf * lax.rsqrt(ms + EPS) * gain


def _norm_into(x_ref, g_ref, h_ref):
    gain = g_ref[...]

    def body(r, carry):
        rows = pl.ds(pl.multiple_of(r * NORM_ROWS, NORM_ROWS), NORM_ROWS)
        h_ref[rows, :] = _rms_rows(x_ref[rows, :], gain).astype(h_ref.dtype)
        return carry

    lax.fori_loop(0, x_ref.shape[0] // NORM_ROWS, body, 0)


def _rope_table_kernel(pos_ref, inv_r_ref, inv_a_ref, rot_ref, sa_ref, sb_ref,
                       rc_ref, rs_ref, ac_ref, asa_ref, asb_ref):
    pos = pos_ref[...]
    ang_r = pos * inv_r_ref[...]
    rc_ref[...] = jnp.cos(ang_r)
    rs_ref[...] = jnp.sin(ang_r)
    ang_a = pos * inv_a_ref[...]
    ca = jnp.cos(ang_a)
    sa = jnp.sin(ang_a)
    ac_ref[...] = jnp.where(rot_ref[...] > 0.5, ca, 1.0)
    asa_ref[...] = sa * sa_ref[...]
    asb_ref[...] = sa * sb_ref[...]


def _rope_tables(pos_b):
    m = pos_b.shape[0]
    half_r = RET_HEAD_DIM // 2
    inv_r = jnp.power(F32(RET_THETA), -jnp.arange(half_r, dtype=F32) / half_r)
    half_a = ROPE_DIM // 2
    inv_a8 = jnp.power(F32(ROPE_THETA), -jnp.arange(half_a, dtype=F32) / half_a)
    lane = jnp.arange(LANES) % HEAD_DIM
    inv_a = jnp.where(lane < ROPE_DIM, inv_a8[lane % half_a], 0.0).astype(F32)
    rot = (lane < ROPE_DIM).astype(F32)
    sa = jnp.where(lane < half_a, -1.0, 0.0).astype(F32)
    sb = jnp.where((lane >= half_a) & (lane < ROPE_DIM), 1.0, 0.0).astype(F32)
    rows = 2048
    row_spec = pl.BlockSpec((rows, LANES), lambda i: (i, 0))
    lane_spec = pl.BlockSpec((1, LANES), lambda i: (0, 0))
    return pl.pallas_call(
        _rope_table_kernel,
        grid=(m // rows,),
        in_specs=[row_spec] + [lane_spec] * 5,
        out_specs=[row_spec] * 5,
        out_shape=[jax.ShapeDtypeStruct((m, LANES), F32)] * 5,
        compiler_params=pltpu.CompilerParams(
            dimension_semantics=("parallel",), vmem_limit_bytes=VMEM_LIMIT),
        name="rope_tables",
    )(pos_b, inv_r[None], inv_a[None], rot[None], sa[None], sb[None])


def _in_proj_kernel(x_ref, g_ref, w_ref, ac_ref, asa_ref, asb_ref, rc_ref, rs_ref, dec_ref,
                    o_ref, h_ref):
    j = pl.program_id(1)
    tm = x_ref.shape[0]

    @pl.when(j == 0)
    def _():
        _norm_into(x_ref, g_ref, h_ref)

    def project():
        return jnp.dot(h_ref[...], w_ref[...], preferred_element_type=F32)

    def attn_rope(acc, n_cols, scale):
        cos, sin_a, sin_b = ac_ref[...], asa_ref[...], asb_ref[...]
        for c in range(n_cols):
            cols = slice(c * LANES, (c + 1) * LANES)
            xc = acc[:, cols]
            y = (xc * cos + pltpu.roll(xc, LANES - ROPE_DIM // 2, 1) * sin_a
                 + pltpu.roll(xc, ROPE_DIM // 2, 1) * sin_b)
            if scale is not None:
                y = y * scale
            o_ref[:, cols] = y.astype(o_ref.dtype)

    @pl.when(j < Q_TILES)
    def _():
        attn_rope(project(), TN_IN // LANES, HEAD_DIM ** -0.5 * LOG2E)

    @pl.when(j == KV_TILE)
    def _():
        acc = project()
        attn_rope(acc, D_KV // LANES, None)
        o_ref[:, D_KV:] = acc[:, D_KV:].astype(o_ref.dtype)

    @pl.when((j >= ROPE_R_TILES[0]) & (j < ROPE_R_TILES[1]))
    def _():
        acc = project()
        half = RET_HEAD_DIM // 2
        first = (j - ROPE_R_TILES[0]) * HEADS_PER_TILE
        cos, sin = rc_ref[...], rs_ref[...]
        for hh in range(HEADS_PER_TILE):
            dec = jnp.concatenate([dec_ref[first + hh]] * (tm // BLOCK), axis=0)
            lo = slice(hh * RET_HEAD_DIM, hh * RET_HEAD_DIM + half)
            hi = slice(hh * RET_HEAD_DIM + half, (hh + 1) * RET_HEAD_DIM)
            x1 = acc[:, lo]
            x2 = acc[:, hi]
            o_ref[:, lo] = ((x1 * cos - x2 * sin) * dec).astype(o_ref.dtype)
            o_ref[:, hi] = ((x2 * cos + x1 * sin) * dec).astype(o_ref.dtype)

    @pl.when(j >= ROPE_R_TILES[1])
    def _():
        o_ref[...] = project().astype(o_ref.dtype)


def _in_proj(x, gain, w, layer, tables, dec):
    m, d = x.shape
    n = w.shape[2]
    rc, rs, ac, asa, asb = tables
    tab_spec = pl.BlockSpec((TM_IN, LANES), lambda i, j: (i, 0))
    return pl.pallas_call(
        _in_proj_kernel,
        grid=(m // TM_IN, n // TN_IN),
        in_specs=[pl.BlockSpec((TM_IN, d), lambda i, j: (i, 0)),
                  pl.BlockSpec((None, 1, d), lambda i, j: (layer, 0, 0)),
                  pl.BlockSpec((None, d, TN_IN), lambda i, j: (layer, 0, j)),
                  tab_spec, tab_spec, tab_spec, tab_spec, tab_spec,
                  pl.BlockSpec(dec.shape, lambda i, j: (0, 0, 0))],
        out_specs=pl.BlockSpec((TM_IN, TN_IN), lambda i, j: (i, j)),
        out_shape=jax.ShapeDtypeStruct((m, n), BF16),
        scratch_shapes=[pltpu.VMEM((TM_IN, d), BF16)],
        compiler_params=pltpu.CompilerParams(
            dimension_semantics=("parallel", "arbitrary"),
            vmem_limit_bytes=VMEM_LIMIT),
        name="norm_in_proj_rope",
    )(x, gain, w, ac, asa, asb, rc, rs, dec)


def _mixer_kernel(qa_ref, kc_ref, kp_ref, vc_ref, vp_ref,
                  qr0_ref, qr1_ref, kr0_ref, kr1_ref, vr0_ref, vr1_ref, gr0_ref, gr1_ref,
                  sink_ref, cdec_ref, anorm_ref, rnorm_ref,
                  o_ref, state_ref, ya_ref):
    n = pl.program_id(1)
    qr_refs, kr_refs = (qr0_ref, qr1_ref), (kr0_ref, kr1_ref)
    vr_refs, gr_refs = (vr0_ref, vr1_ref), (gr0_ref, gr1_ref)

    @pl.when(n == 0)
    def _():
        state_ref[...] = jnp.zeros_like(state_ref)

    band = 2 * BLOCK
    lane = lax.broadcasted_iota(jnp.int32, (band, LANES), 1)
    low = lane < HEAD_DIM
    ones_lo = jnp.where(low, 1.0, 0.0).astype(BF16)
    ones_hi = jnp.where(low, 0.0, 1.0).astype(BF16)

    def split_heads(prev_ref, cur_ref, c):
        cols = slice(c * LANES, (c + 1) * LANES)
        x = jnp.concatenate([prev_ref[:, cols], cur_ref[:, cols]], axis=0).astype(F32)
        xr = pltpu.roll(x, HEAD_DIM, 1)
        even = (jnp.where(low, x, 0.0).astype(BF16), jnp.where(low, 0.0, xr).astype(BF16))
        odd = (jnp.where(low, xr, 0.0).astype(BF16), jnp.where(low, 0.0, x).astype(BF16))
        return even, odd

    k_lo, k_hi, v_lo, v_hi = [], [], [], []
    for c in range(D_KV // LANES):
        for (ka, kb), (va, vb) in zip(split_heads(kp_ref, kc_ref, c),
                                      split_heads(vp_ref, vc_ref, c)):
            k_lo.append(ka)
            k_hi.append(kb)
            v_lo.append(jnp.concatenate([va, ones_lo], axis=1))
            v_hi.append(jnp.concatenate([vb, ones_hi], axis=1))

    qi = lax.broadcasted_iota(jnp.int32, (BLOCK, band), 0)
    kj = lax.broadcasted_iota(jnp.int32, (BLOCK, band), 1)
    diff = BLOCK + qi - kj
    first_key = jnp.where(n > 0, 0, BLOCK)
    valid = (diff >= 0) & (diff < BLOCK) & (kj >= first_key)
    low_q = lax.broadcasted_iota(jnp.int32, (BLOCK, LANES), 1) < HEAD_DIM
    qk_dims = (((1,), (1,)), ((), ()))

    for c in range(D_ATTN // LANES):
        g = (2 * c) // Q_PER_KV
        qc = qa_ref[:, c * LANES:(c + 1) * LANES]
        acc = None
        sink_terms = []
        for hh, (kmat, vmat) in enumerate(((k_lo[g], v_lo[g]), (k_hi[g], v_hi[g]))):
            s = lax.dot_general(qc, kmat, qk_dims, preferred_element_type=F32)
            s = jnp.where(valid, s, NEG_INF)
            sink = sink_ref[2 * c + hh] * LOG2E
            mx = jnp.maximum(jnp.max(s, axis=-1, keepdims=True), sink)
            p = jnp.exp2(s - mx).astype(BF16)
            od = jnp.dot(p, vmat, preferred_element_type=F32)
            acc = od if acc is None else acc + od
            sink_terms.append(jnp.exp2(sink - mx))
        denom = acc[:, LANES:] + jnp.where(low_q, sink_terms[0], sink_terms[1])
        ya_ref[:, c * LANES:(c + 1) * LANES] = acc[:, :LANES] / denom

    o_ref[:, :D_ATTN] = _rms_rows(ya_ref[...], anorm_ref[...]).astype(o_ref.dtype)

    ri = lax.broadcasted_iota(jnp.int32, (BLOCK, BLOCK), 0)
    rj = lax.broadcasted_iota(jnp.int32, (BLOCK, BLOCK), 1)
    causal = ri >= rj
    for h in range(RET_HEADS):
        cols = slice(h * RET_HEAD_DIM, (h + 1) * RET_HEAD_DIM)
        pair, side = divmod(h, HEADS_PER_TILE)
        pcols = slice(side * RET_HEAD_DIM, (side + 1) * RET_HEAD_DIM)
        q = qr_refs[pair][:, pcols]
        k = kr_refs[pair][:, pcols]
        v = vr_refs[pair][:, pcols]
        sc = lax.dot_general(q, k, qk_dims, preferred_element_type=F32)
        sc = jnp.where(causal, sc, 0.0).astype(BF16)
        state = state_ref[h]
        y = (jnp.dot(sc, v, preferred_element_type=F32)
             + jnp.dot(q, state.astype(BF16), preferred_element_type=F32))
        kv = lax.dot_general(k, v, (((0,), (0,)), ((), ())), preferred_element_type=F32)
        state_ref[h] = (state + kv) * cdec_ref[h]
        y = _rms_rows(y, rnorm_ref[:, cols])
        gate = gr_refs[pair][:, pcols].astype(F32)
        o_ref[:, D_ATTN + h * RET_HEAD_DIM:D_ATTN + (h + 1) * RET_HEAD_DIM] = (
            y * (gate * jax.nn.sigmoid(gate))).astype(o_ref.dtype)


def _mixer(proj, sinks, cdec, anorm, rnorm, layer, batch):
    m = proj.shape[0]
    nb = m // batch // BLOCK

    def rows(width, col):
        return pl.BlockSpec((BLOCK, width), lambda b, n: (b * nb + n, col))

    def prev_rows(width, col):
        return pl.BlockSpec((BLOCK, width), lambda b, n: (b * nb + jnp.maximum(n - 1, 0), col))

    k_col = D_ATTN // D_KV
    ret_specs = [rows(TN_IN, RET_TILE0 + t) for t in range(4 * D_RET // TN_IN)]
    smem = pl.BlockSpec(memory_space=pltpu.SMEM)
    return pl.pallas_call(
        _mixer_kernel,
        grid=(batch, nb),
        in_specs=[rows(D_ATTN, 0),
                  rows(D_KV, k_col), prev_rows(D_KV, k_col),
                  rows(D_KV, k_col + 1), prev_rows(D_KV, k_col + 1),
                  *ret_specs,
                  smem, smem,
                  pl.BlockSpec((None, 1, D_ATTN), lambda b, n: (layer, 0, 0)),
                  pl.BlockSpec((None, 1, D_RET), lambda b, n: (layer, 0, 0))],
        out_specs=rows(D_ATTN + D_RET, 0),
        out_shape=jax.ShapeDtypeStruct((m, D_ATTN + D_RET), BF16),
        scratch_shapes=[pltpu.VMEM((RET_HEADS, RET_HEAD_DIM, RET_HEAD_DIM), F32),
                        pltpu.VMEM((BLOCK, D_ATTN), F32)],
        compiler_params=pltpu.CompilerParams(
            dimension_semantics=("parallel", "arbitrary"),
            vmem_limit_bytes=VMEM_LIMIT),
        name="token_mixer",
    )(*([proj] * (5 + len(ret_specs))), sinks, cdec, anorm, rnorm)


def _matmul_norm_res_kernel(a_ref, w_ref, x_ref, g_ref, o_ref):
    y = jnp.dot(a_ref[...], w_ref[...], preferred_element_type=F32)
    o_ref[...] = x_ref[...] + _rms_rows(y, g_ref[...])


def _matmul_norm_res(a, w, x, gain, layer):
    m, k = a.shape
    d = w.shape[2]
    return pl.pallas_call(
        _matmul_norm_res_kernel,
        grid=(m // TM_OUT,),
        in_specs=[pl.BlockSpec((TM_OUT, k), lambda i: (i, 0)),
                  pl.BlockSpec((None, k, d), lambda i: (layer, 0, 0),
                               pipeline_mode=pl.Buffered(1)),
                  pl.BlockSpec((TM_OUT, d), lambda i: (i, 0)),
                  pl.BlockSpec((None, 1, d), lambda i: (layer, 0, 0))],
        out_specs=pl.BlockSpec((TM_OUT, d), lambda i: (i, 0)),
        out_shape=jax.ShapeDtypeStruct((m, d), F32),
        compiler_params=pltpu.CompilerParams(
            dimension_semantics=("parallel",), vmem_limit_bytes=VMEM_LIMIT),
        name="proj_norm_residual",
    )(a, w, x, gain)


def _ffn_up_kernel(tiles_per_seq, x_ref, g_ref, wa_ref, wg_ref, cwa_ref, cwg_ref,
                   cba_ref, cbg_ref, o_ref, h_ref, ua_ref, ug_ref, carry_ref):
    i = pl.program_id(0)
    j = pl.program_id(1)
    tm = x_ref.shape[0]

    @pl.when(j == 0)
    def _():
        _norm_into(x_ref, g_ref, h_ref)

    @pl.when((i == 0) & (j == 0))
    def _():
        carry_ref[...] = jnp.zeros_like(carry_ref)

    starts_sequence = (i % tiles_per_seq) == 0
    h = h_ref[...]

    def conv(u_ref, w_ref, cw, cb, slot):
        u = jnp.dot(h, w_ref[...], preferred_element_type=F32)
        u_ref[0:SUBLANES, :] = jnp.where(starts_sequence, 0.0, carry_ref[slot, j])
        u_ref[SUBLANES:SUBLANES + tm, :] = u
        carry_ref[slot, j] = u[tm - SUBLANES:, :]
        return (cb + u_ref[SUBLANES - 2:SUBLANES - 2 + tm, :] * cw[0:1]
                + u_ref[SUBLANES - 1:SUBLANES - 1 + tm, :] * cw[1:2] + u * cw[2:3])

    a = conv(ua_ref, wa_ref, cwa_ref[...], cba_ref[...], 0)
    half_gate = conv(ug_ref, wg_ref, cwg_ref[...] * 0.5, cbg_ref[...] * 0.5, 1)
    t = jnp.tanh(a * (GELU_C1 + GELU_C2 * (a * a)))
    o_ref[...] = ((a + a * t) * half_gate).astype(o_ref.dtype)


def _ffn_up(x, gain, w_up, conv_w, conv_b, layer, seq):
    m, d = x.shape
    nj = D_FF // TN_IN
    kern = functools.partial(_ffn_up_kernel, seq // TM_IN)
    return pl.pallas_call(
        kern,
        grid=(m // TM_IN, nj),
        in_specs=[pl.BlockSpec((TM_IN, d), lambda i, j: (i, 0)),
                  pl.BlockSpec((None, 1, d), lambda i, j: (layer, 0, 0)),
                  pl.BlockSpec((None, d, TN_IN), lambda i, j: (layer, 0, j)),
                  pl.BlockSpec((None, d, TN_IN), lambda i, j: (layer, 0, j + nj)),
                  pl.BlockSpec((None, CONV_W, TN_IN), lambda i, j: (layer, 0, j)),
                  pl.BlockSpec((None, CONV_W, TN_IN), lambda i, j: (layer, 0, j + nj)),
                  pl.BlockSpec((None, 1, TN_IN), lambda i, j: (layer, 0, j)),
                  pl.BlockSpec((None, 1, TN_IN), lambda i, j: (layer, 0, j + nj))],
        out_specs=pl.BlockSpec((TM_IN, TN_IN), lambda i, j: (i, j)),
        out_shape=jax.ShapeDtypeStruct((m, D_FF), BF16),
        scratch_shapes=[pltpu.VMEM((TM_IN, d), BF16),
                        pltpu.VMEM((TM_IN + SUBLANES, TN_IN), F32),
                        pltpu.VMEM((TM_IN + SUBLANES, TN_IN), F32),
                        pltpu.VMEM((2, nj, SUBLANES, TN_IN), F32)],
        compiler_params=pltpu.CompilerParams(
            dimension_semantics=("arbitrary", "arbitrary"),
            vmem_limit_bytes=VMEM_LIMIT),
        name="ffn_up_conv_gate",
    )(x, gain, w_up, w_up, conv_w, conv_w, conv_b, conv_b)


def _retention_decay_tables():
    lg = jnp.log(1.0 - jnp.power(2.0, -5.0 - jnp.arange(RET_HEADS, dtype=F32)))
    idx = jnp.arange(BLOCK, dtype=F32)
    q_dec = jnp.exp(lg[:, None] * idx)
    k_dec = jnp.exp(-lg[:, None] * idx) * RET_HEAD_DIM ** -0.5
    dec = jnp.concatenate([q_dec, k_dec], axis=0)[..., None]
    dec = jnp.broadcast_to(dec, (2 * RET_HEADS, BLOCK, LANES)).astype(F32)
    chunk_dec = jnp.exp(lg * BLOCK).astype(F32)
    return dec, chunk_dec


def kernel(x, positions, w_in, w_out, w_up, w_down, conv_w, conv_b, attn_sinks,
           pre_mix_norm, post_mix_norm, attn_out_norm, ret_out_norm,
           pre_ffn_norm, post_ffn_norm):
    batch, seq, d = x.shape
    m = batch * seq
    depth = w_in.shape[0]
    xf = x.reshape(m, d)
    pos_b = jnp.broadcast_to(positions.reshape(m, 1).astype(F32), (m, LANES))
    tables = _rope_tables(pos_b)
    dec, cdec = _retention_decay_tables()
    w_in_b = w_in.astype(BF16)
    w_out_b = w_out.astype(BF16)
    w_up_b = w_up.astype(BF16)
    w_down_b = w_down.astype(BF16)
    pre_mix = pre_mix_norm[:, None]
    post_mix = post_mix_norm[:, None]
    pre_ffn = pre_ffn_norm[:, None]
    post_ffn = post_ffn_norm[:, None]
    anorm = attn_out_norm[:, None]
    rnorm = ret_out_norm.reshape(depth, 1, D_RET)
    conv_b3 = conv_b[:, None]
    for l in range(depth):
        proj = _in_proj(xf, pre_mix, w_in_b, l, tables, dec)
        mix = _mixer(proj, attn_sinks[l], cdec, anorm, rnorm, l, batch)
        xf = _matmul_norm_res(mix, w_out_b, xf, post_mix, l)
        f = _ffn_up(xf, pre_ffn, w_up_b, conv_w, conv_b3, l, seq)
        xf = _matmul_norm_res(f, w_down_b, xf, post_ffn, l)
    return xf.reshape(batch, seq, d)
```

```python
import functools

import jax
import jax.numpy as jnp
from jax import lax
from jax.experimental import pallas as pl
from jax.experimental.pallas import tpu as pltpu

F32 = jnp.float32
BF16 = jnp.bfloat16

D_MODEL = 2048
D_ATTN = 1024
HEAD_DIM = 64
N_Q_HEADS = 16
N_KV_HEADS = 4
Q_PER_KV = N_Q_HEADS // N_KV_HEADS
BLOCK = 128
ROPE_DIM = 16
ROPE_THETA = 500000.0
D_RET = 1024
RET_HEADS = 4
RET_HEAD_DIM = 256
RET_THETA = 10000.0
D_KV = N_KV_HEADS * HEAD_DIM
D_IN = D_ATTN + 2 * D_KV + 4 * D_RET
D_FF = 5632
CONV_W = 3
EPS = 1e-6
NEG_INF = -1e30
LOG2E = 1.4426950408889634

LANES = 128
SUBLANES = 8
VMEM_LIMIT = 56 * 1024 * 1024

TM_IN = 1024
TN_IN = 512
TM_OUT = 256
NORM_ROWS = 128
MIX_ROWS = 256
GELU_C1 = 0.7978845608028654
GELU_C2 = GELU_C1 * 0.044715

Q_TILES = D_ATTN // TN_IN
KV_TILE = Q_TILES
RET_TILE0 = (D_ATTN + 2 * D_KV) // TN_IN
ROPE_R_TILES = (RET_TILE0, RET_TILE0 + 2 * D_RET // TN_IN)
HEADS_PER_TILE = TN_IN // RET_HEAD_DIM
assert 2 * D_KV == TN_IN and (D_ATTN + 2 * D_KV) % TN_IN == 0


def _rms_rows(xf, gain):
    ms = jnp.mean(xf * xf, axis=-1, keepdims=True)
    return xf * lax.rsqrt(ms + EPS) * gain


def _norm_into(x_ref, g_ref, h_ref):
    gain = g_ref[...]

    def body(r, carry):
        rows = pl.ds(pl.multiple_of(r * NORM_ROWS, NORM_ROWS), NORM_ROWS)
        h_ref[rows, :] = _rms_rows(x_ref[rows, :], gain).astype(h_ref.dtype)
        return carry

    lax.fori_loop(0, x_ref.shape[0] // NORM_ROWS, body, 0)


def _rope_table_kernel(pos_ref, inv_r_ref, inv_a_ref, rot_ref, sa_ref, sb_ref,
                       rc_ref, rs_ref, ac_ref, asa_ref, asb_ref):
    pos = pos_ref[...]
    ang_r = pos * inv_r_ref[...]
    rc_ref[...] = jnp.cos(ang_r)
    rs_ref[...] = jnp.sin(ang_r)
    ang_a = pos * inv_a_ref[...]
    ca = jnp.cos(ang_a)
    sa = jnp.sin(ang_a)
    ac_ref[...] = jnp.where(rot_ref[...] > 0.5, ca, 1.0)
    asa_ref[...] = sa * sa_ref[...]
    asb_ref[...] = sa * sb_ref[...]


def _rope_tables(pos_b):
    m = pos_b.shape[0]
    half_r = RET_HEAD_DIM // 2
    inv_r = jnp.power(F32(RET_THETA), -jnp.arange(half_r, dtype=F32) / half_r)
    half_a = ROPE_DIM // 2
    inv_a8 = jnp.power(F32(ROPE_THETA), -jnp.arange(half_a, dtype=F32) / half_a)
    lane = jnp.arange(LANES) % HEAD_DIM
    inv_a = jnp.where(lane < ROPE_DIM, inv_a8[lane % half_a], 0.0).astype(F32)
    rot = (lane < ROPE_DIM).astype(F32)
    sa = jnp.where(lane < half_a, -1.0, 0.0).astype(F32)
    sb = jnp.where((lane >= half_a) & (lane < ROPE_DIM), 1.0, 0.0).astype(F32)
    rows = 2048
    row_spec = pl.BlockSpec((rows, LANES), lambda i: (i, 0))
    lane_spec = pl.BlockSpec((1, LANES), lambda i: (0, 0))
    return pl.pallas_call(
        _rope_table_kernel,
        grid=(m // rows,),
        in_specs=[row_spec] + [lane_spec] * 5,
        out_specs=[row_spec] * 5,
        out_shape=[jax.ShapeDtypeStruct((m, LANES), F32)] * 5,
        compiler_params=pltpu.CompilerParams(
            dimension_semantics=("parallel",), vmem_limit_bytes=VMEM_LIMIT),
        name="rope_tables",
    )(pos_b, inv_r[None], inv_a[None], rot[None], sa[None], sb[None])


def _in_proj_kernel(x_ref, g_ref, w_ref, ac_ref, asa_ref, asb_ref, rc_ref, rs_ref, dec_ref,
                    o_ref):
    tm = x_ref.shape[0]
    h = _rms_rows(x_ref[...], g_ref[...]).astype(BF16)

    def attn_rope(acc, base, n_cols, scale):
        cos, sin_a, sin_b = ac_ref[...], asa_ref[...], asb_ref[...]
        for c in range(n_cols):
            xc = acc[:, c * LANES:(c + 1) * LANES]
            y = (xc * cos + pltpu.roll(xc, LANES - ROPE_DIM // 2, 1) * sin_a
                 + pltpu.roll(xc, ROPE_DIM // 2, 1) * sin_b)
            if scale is not None:
                y = y * scale
            o_ref[:, base + c * LANES:base + (c + 1) * LANES] = y.astype(o_ref.dtype)

    for t in range(w_ref.shape[1] // TN_IN):
        base = t * TN_IN
        acc = jnp.dot(h, w_ref[:, base:base + TN_IN], preferred_element_type=F32)
        if t < Q_TILES:
            attn_rope(acc, base, TN_IN // LANES, HEAD_DIM ** -0.5 * LOG2E)
        elif t == KV_TILE:
            attn_rope(acc, base, D_KV // LANES, None)
            o_ref[:, base + D_KV:base + TN_IN] = acc[:, D_KV:].astype(o_ref.dtype)
        elif t < ROPE_R_TILES[1]:
            half = RET_HEAD_DIM // 2
            cos, sin = rc_ref[...], rs_ref[...]
            for hh in range(HEADS_PER_TILE):
                head = (t - ROPE_R_TILES[0]) * HEADS_PER_TILE + hh
                dec = jnp.concatenate([dec_ref[head]] * (tm // BLOCK), axis=0)
                lo = hh * RET_HEAD_DIM
                x1 = acc[:, lo:lo + half]
                x2 = acc[:, lo + half:lo + 2 * half]
                o_ref[:, base + lo:base + lo + half] = (
                    (x1 * cos - x2 * sin) * dec).astype(o_ref.dtype)
                o_ref[:, base + lo + half:base + lo + 2 * half] = (
                    (x2 * cos + x1 * sin) * dec).astype(o_ref.dtype)
        else:
            o_ref[:, base:base + TN_IN] = acc.astype(o_ref.dtype)


def _in_proj(x, gain, w, layer, tables, dec):
    m, d = x.shape
    n = w.shape[2]
    rc, rs, ac, asa, asb = tables
    tab_spec = pl.BlockSpec((TM_OUT, LANES), lambda i: (i, 0))
    return pl.pallas_call(
        _in_proj_kernel,
        grid=(m // TM_OUT,),
        in_specs=[pl.BlockSpec((TM_OUT, d), lambda i: (i, 0)),
                  pl.BlockSpec((None, 1, d), lambda i: (layer, 0, 0)),
                  pl.BlockSpec((None, d, n), lambda i: (layer, 0, 0),
                               pipeline_mode=pl.Buffered(1)),
                  tab_spec, tab_spec, tab_spec, tab_spec, tab_spec,
                  pl.BlockSpec(dec.shape, lambda i: (0, 0, 0))],
        out_specs=pl.BlockSpec((TM_OUT, n), lambda i: (i, 0)),
        out_shape=jax.ShapeDtypeStruct((m, n), BF16),
        compiler_params=pltpu.CompilerParams(
            dimension_semantics=("parallel",), vmem_limit_bytes=VMEM_LIMIT),
        name="norm_in_proj_rope",
    )(x, gain, w, ac, asa, asb, rc, rs, dec)


def _mixer_kernel(qa_ref, kc_ref, kp_ref, vc_ref, vp_ref,
                  qr0_ref, qr1_ref, kr0_ref, kr1_ref, vr0_ref, vr1_ref, gr0_ref, gr1_ref,
                  sink_ref, cdec_ref, anorm_ref, rnorm_ref,
                  o_ref, state_ref, ya_ref):
    n = pl.program_id(1)
    qr_refs, kr_refs = (qr0_ref, qr1_ref), (kr0_ref, kr1_ref)
    vr_refs, gr_refs = (vr0_ref, vr1_ref), (gr0_ref, gr1_ref)

    @pl.when(n == 0)
    def _():
        state_ref[...] = jnp.zeros_like(state_ref)

    lane_b = lax.broadcasted_iota(jnp.int32, (BLOCK, LANES), 1)
    low_b = lane_b < HEAD_DIM
    ones_lo = jnp.where(low_b, 1.0, 0.0).astype(BF16)
    ones_hi = jnp.where(low_b, 0.0, 1.0).astype(BF16)

    def placed(x_b):
        x = x_b.astype(F32)
        xr = pltpu.roll(x, HEAD_DIM, 1)
        views = ((jnp.where(low_b, x, 0.0), jnp.where(low_b, 0.0, xr)),
                 (jnp.where(low_b, xr, 0.0), jnp.where(low_b, 0.0, x)))
        return [tuple(t.astype(BF16) for t in pair) for pair in views]

    def kv_matrices(k_ref, v_ref, rows):
        ks, vs = [], []
        for c in range(D_KV // LANES):
            cols = slice(c * LANES, (c + 1) * LANES)
            for k2, (lo, hi) in zip(placed(k_ref[rows, cols]), placed(v_ref[rows, cols])):
                ks.append(jnp.concatenate(k2, axis=0))
                vs.append(jnp.concatenate([jnp.concatenate([lo, ones_lo], axis=1),
                                           jnp.concatenate([hi, ones_hi], axis=1)], axis=0))
        return ks, vs

    n_sub = qa_ref.shape[0] // BLOCK
    blocks = [kv_matrices(kp_ref, vp_ref, slice(0, BLOCK))]
    blocks += [kv_matrices(kc_ref, vc_ref, slice(sb * BLOCK, (sb + 1) * BLOCK))
               for sb in range(n_sub)]

    qi = lax.broadcasted_iota(jnp.int32, (BLOCK, 2 * BLOCK), 0)
    kj = lax.broadcasted_iota(jnp.int32, (BLOCK, 2 * BLOCK), 1) & (BLOCK - 1)
    in_cur = kj <= qi
    no_prev = jnp.where(n > 0, 0.0, NEG_INF)
    qk_dims = (((1,), (1,)), ((), ()))

    for sb in range(n_sub):
        rows = slice(sb * BLOCK, (sb + 1) * BLOCK)
        (k_prev, v_prev), (k_cur, v_cur) = blocks[sb], blocks[sb + 1]
        for c in range(D_ATTN // LANES):
            g = (2 * c) // Q_PER_KV
            qc = qa_ref[rows, c * LANES:(c + 1) * LANES]
            s_cur = lax.dot_general(qc, k_cur[g], qk_dims, preferred_element_type=F32)
            s_prev = lax.dot_general(qc, k_prev[g], qk_dims, preferred_element_type=F32)
            if sb == 0:
                s_prev = s_prev + no_prev
            s = jnp.where(in_cur, s_cur, s_prev)
            ps, sink_terms = [], []
            for hh in range(2):
                sh = s[:, hh * BLOCK:(hh + 1) * BLOCK]
                sink = sink_ref[2 * c + hh] * LOG2E
                mx = jnp.maximum(jnp.max(sh, axis=-1, keepdims=True), sink)
                ps.append(jnp.exp2(sh - mx))
                sink_terms.append(jnp.exp2(sink - mx))
            p = jnp.concatenate(ps, axis=1)
            acc = (jnp.dot(jnp.where(in_cur, p, 0.0).astype(BF16), v_cur[g],
                           preferred_element_type=F32)
                   + jnp.dot(jnp.where(in_cur, 0.0, p).astype(BF16), v_prev[g],
                             preferred_element_type=F32))
            denom = acc[:, LANES:] + jnp.where(low_b, sink_terms[0], sink_terms[1])
            ya_ref[rows, c * LANES:(c + 1) * LANES] = acc[:, :LANES] / denom

    o_ref[:, :D_ATTN] = _rms_rows(ya_ref[...], anorm_ref[...]).astype(o_ref.dtype)

    ri = lax.broadcasted_iota(jnp.int32, (BLOCK, BLOCK), 0)
    rj = lax.broadcasted_iota(jnp.int32, (BLOCK, BLOCK), 1)
    causal = ri >= rj
    for h in range(RET_HEADS):
        cols = slice(h * RET_HEAD_DIM, (h + 1) * RET_HEAD_DIM)
        pair, side = divmod(h, HEADS_PER_TILE)
        pcols = slice(side * RET_HEAD_DIM, (side + 1) * RET_HEAD_DIM)
        state = state_ref[h]
        for sb in range(n_sub):
            rows = slice(sb * BLOCK, (sb + 1) * BLOCK)
            q = qr_refs[pair][rows, pcols]
            k = kr_refs[pair][rows, pcols]
            v = vr_refs[pair][rows, pcols]
            sc = lax.dot_general(q, k, qk_dims, preferred_element_type=F32)
            sc = jnp.where(causal, sc, 0.0).astype(BF16)
            y = (jnp.dot(sc, v, preferred_element_type=F32)
                 + jnp.dot(q, state.astype(BF16), preferred_element_type=F32))
            kv = lax.dot_general(k, v, (((0,), (0,)), ((), ())), preferred_element_type=F32)
            state = (state + kv) * cdec_ref[h]
            y = _rms_rows(y, rnorm_ref[:, cols])
            gate = gr_refs[pair][rows, pcols].astype(F32)
            o_ref[rows, D_ATTN + h * RET_HEAD_DIM:D_ATTN + (h + 1) * RET_HEAD_DIM] = (
                y * (gate * jax.nn.sigmoid(gate))).astype(o_ref.dtype)
        state_ref[h] = state


def _mixer(proj, sinks, cdec, anorm, rnorm, layer, batch):
    m = proj.shape[0]
    steps = m // batch // MIX_ROWS
    sub = MIX_ROWS // BLOCK

    def rows(width, col):
        return pl.BlockSpec((MIX_ROWS, width), lambda b, n: (b * steps + n, col))

    def prev_rows(width, col):
        return pl.BlockSpec(
            (BLOCK, width), lambda b, n: ((b * steps + n) * sub - jnp.minimum(n, 1), col))

    k_col = D_ATTN // D_KV
    ret_specs = [rows(TN_IN, RET_TILE0 + t) for t in range(4 * D_RET // TN_IN)]
    smem = pl.BlockSpec(memory_space=pltpu.SMEM)
    return pl.pallas_call(
        _mixer_kernel,
        grid=(batch, steps),
        in_specs=[rows(D_ATTN, 0),
                  rows(D_KV, k_col), prev_rows(D_KV, k_col),
                  rows(D_KV, k_col + 1), prev_rows(D_KV, k_col + 1),
                  *ret_specs,
                  smem, smem,
                  pl.BlockSpec((None, 1, D_ATTN), lambda b, n: (layer, 0, 0)),
                  pl.BlockSpec((None, 1, D_RET), lambda b, n: (layer, 0, 0))],
        out_specs=rows(D_ATTN + D_RET, 0),
        out_shape=jax.ShapeDtypeStruct((m, D_ATTN + D_RET), BF16),
        scratch_shapes=[pltpu.VMEM((RET_HEADS, RET_HEAD_DIM, RET_HEAD_DIM), F32),
                        pltpu.VMEM((MIX_ROWS, D_ATTN), F32)],
        compiler_params=pltpu.CompilerParams(
            dimension_semantics=("parallel", "arbitrary"),
            vmem_limit_bytes=VMEM_LIMIT),
        name="token_mixer",
    )(*([proj] * (5 + len(ret_specs))), sinks, cdec, anorm, rnorm)


def _matmul_norm_res_kernel(a_ref, w_ref, x_ref, g_ref, o_ref):
    y = jnp.dot(a_ref[...], w_ref[...], preferred_element_type=F32)
    o_ref[...] = x_ref[...] + _rms_rows(y, g_ref[...])


def _matmul_norm_res(a, w, x, gain, layer):
    m, k = a.shape
    d = w.shape[2]
    return pl.pallas_call(
        _matmul_norm_res_kernel,
        grid=(m // TM_OUT,),
        in_specs=[pl.BlockSpec((TM_OUT, k), lambda i: (i, 0)),
                  pl.BlockSpec((None, k, d), lambda i: (layer, 0, 0),
                               pipeline_mode=pl.Buffered(1)),
                  pl.BlockSpec((TM_OUT, d), lambda i: (i, 0)),
                  pl.BlockSpec((None, 1, d), lambda i: (layer, 0, 0))],
        out_specs=pl.BlockSpec((TM_OUT, d), lambda i: (i, 0)),
        out_shape=jax.ShapeDtypeStruct((m, d), F32),
        compiler_params=pltpu.CompilerParams(
            dimension_semantics=("parallel",), vmem_limit_bytes=VMEM_LIMIT),
        name="proj_norm_residual",
    )(a, w, x, gain)


def _ffn_up_kernel(tiles_per_seq, x_ref, g_ref, wa_ref, wg_ref, cwa_ref, cwg_ref,
                   cba_ref, cbg_ref, o_ref, h_ref, ua_ref, ug_ref, carry_ref):
    i = pl.program_id(0)
    j = pl.program_id(1)
    tm = x_ref.shape[0]

    @pl.when(j == 0)
    def _():
        _norm_into(x_ref, g_ref, h_ref)

    @pl.when((i == 0) & (j == 0))
    def _():
        carry_ref[...] = jnp.zeros_like(carry_ref)

    starts_sequence = (i % tiles_per_seq) == 0
    h = h_ref[...]

    def conv(u_ref, w_ref, cw, cb, slot):
        u = jnp.dot(h, w_ref[...], preferred_element_type=F32)
        u_ref[0:SUBLANES, :] = jnp.where(starts_sequence, 0.0, carry_ref[slot, j])
        u_ref[SUBLANES:SUBLANES + tm, :] = u
        carry_ref[slot, j] = u[tm - SUBLANES:, :]
        return (cb + u_ref[SUBLANES - 2:SUBLANES - 2 + tm, :] * cw[0:1]
                + u_ref[SUBLANES - 1:SUBLANES - 1 + tm, :] * cw[1:2] + u * cw[2:3])

    a = conv(ua_ref, wa_ref, cwa_ref[...], cba_ref[...], 0)
    half_gate = conv(ug_ref, wg_ref, cwg_ref[...] * 0.5, cbg_ref[...] * 0.5, 1)
    t = jnp.tanh(a * (GELU_C1 + GELU_C2 * (a * a)))
    o_ref[...] = ((a + a * t) * half_gate).astype(o_ref.dtype)


def _ffn_up(x, gain, w_up, conv_w, conv_b, layer, seq):
    m, d = x.shape
    nj = D_FF // TN_IN
    kern = functools.partial(_ffn_up_kernel, seq // TM_IN)
    return pl.pallas_call(
        kern,
        grid=(m // TM_IN, nj),
        in_specs=[pl.BlockSpec((TM_IN, d), lambda i, j: (i, 0)),
                  pl.BlockSpec((None, 1, d), lambda i, j: (layer, 0, 0)),
                  pl.BlockSpec((None, d, TN_IN), lambda i, j: (layer, 0, j)),
                  pl.BlockSpec((None, d, TN_IN), lambda i, j: (layer, 0, j + nj)),
                  pl.BlockSpec((None, CONV_W, TN_IN), lambda i, j: (layer, 0, j)),
                  pl.BlockSpec((None, CONV_W, TN_IN), lambda i, j: (layer, 0, j + nj)),
                  pl.BlockSpec((None, 1, TN_IN), lambda i, j: (layer, 0, j)),
                  pl.BlockSpec((None, 1, TN_IN), lambda i, j: (layer, 0, j + nj))],
        out_specs=pl.BlockSpec((TM_IN, TN_IN), lambda i, j: (i, j)),
        out_shape=jax.ShapeDtypeStruct((m, D_FF), BF16),
        scratch_shapes=[pltpu.VMEM((TM_IN, d), BF16),
                        pltpu.VMEM((TM_IN + SUBLANES, TN_IN), F32),
                        pltpu.VMEM((TM_IN + SUBLANES, TN_IN), F32),
                        pltpu.VMEM((2, nj, SUBLANES, TN_IN), F32)],
        compiler_params=pltpu.CompilerParams(
            dimension_semantics=("arbitrary", "arbitrary"),
            vmem_limit_bytes=VMEM_LIMIT),
        name="ffn_up_conv_gate",
    )(x, gain, w_up, w_up, conv_w, conv_w, conv_b, conv_b)


def _retention_decay_tables():
    lg = jnp.log(1.0 - jnp.power(2.0, -5.0 - jnp.arange(RET_HEADS, dtype=F32)))
    idx = jnp.arange(BLOCK, dtype=F32)
    q_dec = jnp.exp(lg[:, None] * idx)
    k_dec = jnp.exp(-lg[:, None] * idx) * RET_HEAD_DIM ** -0.5
    dec = jnp.concatenate([q_dec, k_dec], axis=0)[..., None]
    dec = jnp.broadcast_to(dec, (2 * RET_HEADS, BLOCK, LANES)).astype(F32)
    chunk_dec = jnp.exp(lg * BLOCK).astype(F32)
    return dec, chunk_dec


def kernel(x, positions, w_in, w_out, w_up, w_down, conv_w, conv_b, attn_sinks,
           pre_mix_norm, post_mix_norm, attn_out_norm, ret_out_norm,
           pre_ffn_norm, post_ffn_norm):
    batch, seq, d = x.shape
    m = batch * seq
    depth = w_in.shape[0]
    xf = x.reshape(m, d)
    pos_b = jnp.broadcast_to(positions.reshape(m, 1).astype(F32), (m, LANES))
    tables = _rope_tables(pos_b)
    dec, cdec = _retention_decay_tables()
    w_in_b = w_in.astype(BF16)
    w_out_b = w_out.astype(BF16)
    w_up_b = w_up.astype(BF16)
    w_down_b = w_down.astype(BF16)
    pre_mix = pre_mix_norm[:, None]
    post_mix = post_mix_norm[:, None]
    pre_ffn = pre_ffn_norm[:, None]
    post_ffn = post_ffn_norm[:, None]
    anorm = attn_out_norm[:, None]
    rnorm = ret_out_norm.reshape(depth, 1, D_RET)
    conv_b3 = conv_b[:, None]
    for l in range(depth):
        proj = _in_proj(xf, pre_mix, w_in_b, l, tables, dec)
        mix = _mixer(proj, attn_sinks[l], cdec, anorm, rnorm, l, batch)
        xf = _matmul_norm_res(mix, w_out_b, xf, post_mix, l)
        f = _ffn_up(xf, pre_ffn, w_up_b, conv_w, conv_b3, l, seq)
        xf = _matmul_norm_res(f, w_down_b, xf, post_ffn, l)
    return xf.reshape(batch, seq, d)
```

```python
import functools

import jax
import jax.numpy as jnp
from jax import lax
from jax.experimental import pallas as pl
from jax.experimental.pallas import tpu as pltpu

F32 = jnp.float32
BF16 = jnp.bfloat16

D_MODEL = 2048
D_ATTN = 1024
HEAD_DIM = 64
N_Q_HEADS = 16
N_KV_HEADS = 4
Q_PER_KV = N_Q_HEADS // N_KV_HEADS
BLOCK = 128
ROPE_DIM = 16
ROPE_THETA = 500000.0
D_RET = 1024
RET_HEADS = 4
RET_HEAD_DIM = 256
RET_THETA = 10000.0
D_KV = N_KV_HEADS * HEAD_DIM
D_IN = D_ATTN + 2 * D_KV + 4 * D_RET
D_FF = 5632
CONV_W = 3
EPS = 1e-6
NEG_INF = -1e30
LOG2E = 1.4426950408889634

LANES = 128
SUBLANES = 8
BF16_ROWS = 16
W_CHUNK_BYTES = 4 * 1024 * 1024
VMEM_LIMIT = 56 * 1024 * 1024

TM_IN = 1024
TN_IN = 512
TM_OUT = 256
NORM_ROWS = 128
MIX_ROWS = 256
RET_CHUNK = 256
assert MIX_ROWS % RET_CHUNK == 0 and TM_OUT % RET_CHUNK == 0
GELU_C1 = 0.7978845608028654
GELU_C2 = GELU_C1 * 0.044715

Q_TILES = D_ATTN // TN_IN
KV_TILE = Q_TILES
RET_TILE0 = (D_ATTN + 2 * D_KV) // TN_IN
ROPE_R_TILES = (RET_TILE0, RET_TILE0 + 2 * D_RET // TN_IN)
HEADS_PER_TILE = TN_IN // RET_HEAD_DIM
assert 2 * D_KV == TN_IN and (D_ATTN + 2 * D_KV) % TN_IN == 0


def _rms_rows(xf, gain):
    ms = jnp.mean(xf * xf, axis=-1, keepdims=True)
    return xf * lax.rsqrt(ms + EPS) * gain


def _norm_into(x_ref, g_ref, h_ref):
    gain = g_ref[...]

    def body(r, carry):
        rows = pl.ds(pl.multiple_of(r * NORM_ROWS, NORM_ROWS), NORM_ROWS)
        h_ref[rows, :] = _rms_rows(x_ref[rows, :], gain).astype(h_ref.dtype)
        return carry

    lax.fori_loop(0, x_ref.shape[0] // NORM_ROWS, body, 0)


def _rope_table_kernel(pos_ref, inv_r_ref, inv_a_ref, rot_ref, sa_ref, sb_ref,
                       rc_ref, rs_ref, ac_ref, asa_ref, asb_ref):
    pos = pos_ref[...]
    ang_r = pos * inv_r_ref[...]
    rc_ref[...] = jnp.cos(ang_r)
    rs_ref[...] = jnp.sin(ang_r)
    ang_a = pos * inv_a_ref[...]
    ca = jnp.cos(ang_a)
    sa = jnp.sin(ang_a)
    ac_ref[...] = jnp.where(rot_ref[...] > 0.5, ca, 1.0)
    asa_ref[...] = sa * sa_ref[...]
    asb_ref[...] = sa * sb_ref[...]


def _rope_tables(pos_b):
    m = pos_b.shape[0]
    half_r = RET_HEAD_DIM // 2
    inv_r = jnp.power(F32(RET_THETA), -jnp.arange(half_r, dtype=F32) / half_r)
    half_a = ROPE_DIM // 2
    inv_a8 = jnp.power(F32(ROPE_THETA), -jnp.arange(half_a, dtype=F32) / half_a)
    lane = jnp.arange(LANES) % HEAD_DIM
    inv_a = jnp.where(lane < ROPE_DIM, inv_a8[lane % half_a], 0.0).astype(F32)
    rot = (lane < ROPE_DIM).astype(F32)
    sa = jnp.where(lane < half_a, -1.0, 0.0).astype(F32)
    sb = jnp.where((lane >= half_a) & (lane < ROPE_DIM), 1.0, 0.0).astype(F32)
    rows = 2048
    row_spec = pl.BlockSpec((rows, LANES), lambda i: (i, 0))
    lane_spec = pl.BlockSpec((1, LANES), lambda i: (0, 0))
    return pl.pallas_call(
        _rope_table_kernel,
        grid=(m // rows,),
        in_specs=[row_spec] + [lane_spec] * 5,
        out_specs=[row_spec] * 5,
        out_shape=[jax.ShapeDtypeStruct((m, LANES), F32)] * 5,
        compiler_params=pltpu.CompilerParams(
            dimension_semantics=("parallel",), vmem_limit_bytes=VMEM_LIMIT),
        name="rope_tables",
    )(pos_b, inv_r[None], inv_a[None], rot[None], sa[None], sb[None])


def _in_proj_kernel(n_load, x_ref, g_ref, w_ref, ac_ref, asa_ref, asb_ref, rc_ref, rs_ref,
                    dec_ref, o_ref, wb_ref):
    s = pl.program_id(0)

    @pl.when(s < n_load)
    def _():
        _load_weight_chunk(s, w_ref, wb_ref)

    @pl.when(s >= n_load)
    def _():
        _in_proj_tile(x_ref, g_ref, wb_ref, ac_ref, asa_ref, asb_ref, rc_ref, rs_ref, dec_ref, o_ref)


def _in_proj_tile(x_ref, g_ref, w_ref, ac_ref, asa_ref, asb_ref, rc_ref, rs_ref, dec_ref, o_ref):
    tm = x_ref.shape[0]
    h = _rms_rows(x_ref[...], g_ref[...]).astype(BF16)

    def attn_rope(acc, base, n_cols, scale):
        cos, sin_a, sin_b = ac_ref[...], asa_ref[...], asb_ref[...]
        for c in range(n_cols):
            xc = acc[:, c * LANES:(c + 1) * LANES]
            y = (xc * cos + pltpu.roll(xc, LANES - ROPE_DIM // 2, 1) * sin_a
                 + pltpu.roll(xc, ROPE_DIM // 2, 1) * sin_b)
            if scale is not None:
                y = y * scale
            o_ref[:, base + c * LANES:base + (c + 1) * LANES] = y.astype(o_ref.dtype)

    for t in range(w_ref.shape[1] // TN_IN):
        base = t * TN_IN
        acc = jnp.dot(h, w_ref[:, base:base + TN_IN], preferred_element_type=F32)
        if t < Q_TILES:
            attn_rope(acc, base, TN_IN // LANES, HEAD_DIM ** -0.5 * LOG2E)
        elif t == KV_TILE:
            attn_rope(acc, base, D_KV // LANES, None)
            o_ref[:, base + D_KV:base + TN_IN] = acc[:, D_KV:].astype(o_ref.dtype)
        elif t < ROPE_R_TILES[1]:
            half = RET_HEAD_DIM // 2
            cos, sin = rc_ref[...], rs_ref[...]
            for hh in range(HEADS_PER_TILE):
                head = (t - ROPE_R_TILES[0]) * HEADS_PER_TILE + hh
                dec = jnp.concatenate([dec_ref[head]] * (tm // RET_CHUNK), axis=0)
                lo = hh * RET_HEAD_DIM
                x1 = acc[:, lo:lo + half]
                x2 = acc[:, lo + half:lo + 2 * half]
                o_ref[:, base + lo:base + lo + half] = (
                    (x1 * cos - x2 * sin) * dec).astype(o_ref.dtype)
                o_ref[:, base + lo + half:base + lo + 2 * half] = (
                    (x2 * cos + x1 * sin) * dec).astype(o_ref.dtype)
        else:
            o_ref[:, base:base + TN_IN] = acc.astype(o_ref.dtype)


def _in_proj(x, gain, w, layer, tables, dec):
    m, d = x.shape
    n = w.shape[2]
    rc, rs, ac, asa, asb = tables
    chunk = _weight_chunk_rows(d, n)
    n_load = d // chunk

    def tile(s):
        return jnp.maximum(s - n_load, 0), 0

    tab_spec = pl.BlockSpec((TM_OUT, LANES), tile)
    return pl.pallas_call(
        functools.partial(_in_proj_kernel, n_load),
        grid=(n_load + m // TM_OUT,),
        in_specs=[pl.BlockSpec((TM_OUT, d), tile),
                  pl.BlockSpec((None, 1, d), lambda s: (layer, 0, 0)),
                  pl.BlockSpec((None, chunk, n),
                               lambda s: (layer, jnp.minimum(s, n_load - 1), 0)),
                  tab_spec, tab_spec, tab_spec, tab_spec, tab_spec,
                  pl.BlockSpec(dec.shape, lambda s: (0, 0, 0))],
        out_specs=pl.BlockSpec((TM_OUT, n), tile),
        out_shape=jax.ShapeDtypeStruct((m, n), BF16),
        scratch_shapes=[pltpu.VMEM((d, n), BF16)],
        compiler_params=pltpu.CompilerParams(
            dimension_semantics=("arbitrary",), vmem_limit_bytes=VMEM_LIMIT),
        name="norm_in_proj_rope",
    )(x, gain, w, ac, asa, asb, rc, rs, dec)


def _mixer_kernel(qa_ref, kc_ref, kp_ref, vc_ref, vp_ref,
                  qr0_ref, qr1_ref, kr0_ref, kr1_ref, vr0_ref, vr1_ref, gr0_ref, gr1_ref,
                  sink_ref, cdec_ref, anorm_ref, rnorm_ref,
                  o_ref, state_ref, ya_ref):
    n = pl.program_id(1)
    qr_refs, kr_refs = (qr0_ref, qr1_ref), (kr0_ref, kr1_ref)
    vr_refs, gr_refs = (vr0_ref, vr1_ref), (gr0_ref, gr1_ref)

    @pl.when(n == 0)
    def _():
        state_ref[...] = jnp.zeros_like(state_ref)

    lane_b = lax.broadcasted_iota(jnp.int32, (BLOCK, LANES), 1)
    low_b = lane_b < HEAD_DIM
    ones_lo = jnp.where(low_b, 1.0, 0.0).astype(BF16)
    ones_hi = jnp.where(low_b, 0.0, 1.0).astype(BF16)

    def placed(x_b):
        x = x_b.astype(F32)
        xr = pltpu.roll(x, HEAD_DIM, 1)
        views = ((jnp.where(low_b, x, 0.0), jnp.where(low_b, 0.0, xr)),
                 (jnp.where(low_b, xr, 0.0), jnp.where(low_b, 0.0, x)))
        return [tuple(t.astype(BF16) for t in pair) for pair in views]

    def kv_matrices(k_ref, v_ref, rows):
        ks, vs = [], []
        for c in range(D_KV // LANES):
            cols = slice(c * LANES, (c + 1) * LANES)
            for k2, (lo, hi) in zip(placed(k_ref[rows, cols]), placed(v_ref[rows, cols])):
                ks.append(jnp.concatenate(k2, axis=0))
                vs.append(jnp.concatenate([jnp.concatenate([lo, ones_lo], axis=1),
                                           jnp.concatenate([hi, ones_hi], axis=1)], axis=0))
        return ks, vs

    n_sub = qa_ref.shape[0] // BLOCK
    blocks = [kv_matrices(kp_ref, vp_ref, slice(0, BLOCK))]
    blocks += [kv_matrices(kc_ref, vc_ref, slice(sb * BLOCK, (sb + 1) * BLOCK))
               for sb in range(n_sub)]

    qi = lax.broadcasted_iota(jnp.int32, (BLOCK, 2 * BLOCK), 0)
    kj = lax.broadcasted_iota(jnp.int32, (BLOCK, 2 * BLOCK), 1) & (BLOCK - 1)
    in_cur = kj <= qi
    no_prev = jnp.where(n > 0, 0.0, NEG_INF)
    qk_dims = (((1,), (1,)), ((), ()))

    for sb in range(n_sub):
        rows = slice(sb * BLOCK, (sb + 1) * BLOCK)
        (k_prev, v_prev), (k_cur, v_cur) = blocks[sb], blocks[sb + 1]
        for c in range(D_ATTN // LANES):
            g = (2 * c) // Q_PER_KV
            qc = qa_ref[rows, c * LANES:(c + 1) * LANES]
            s_cur = lax.dot_general(qc, k_cur[g], qk_dims, preferred_element_type=F32)
            s_prev = lax.dot_general(qc, k_prev[g], qk_dims, preferred_element_type=F32)
            if sb == 0:
                s_prev = s_prev + no_prev
            s = jnp.where(in_cur, s_cur, s_prev)
            ps, sink_terms = [], []
            for hh in range(2):
                sh = s[:, hh * BLOCK:(hh + 1) * BLOCK]
                sink = sink_ref[2 * c + hh] * LOG2E
                mx = jnp.maximum(jnp.max(sh, axis=-1, keepdims=True), sink)
                ps.append(jnp.exp2(sh - mx))
                sink_terms.append(jnp.exp2(sink - mx))
            p = jnp.concatenate(ps, axis=1)
            acc = (jnp.dot(jnp.where(in_cur, p, 0.0).astype(BF16), v_cur[g],
                           preferred_element_type=F32)
                   + jnp.dot(jnp.where(in_cur, 0.0, p).astype(BF16), v_prev[g],
                             preferred_element_type=F32))
            denom = acc[:, LANES:] + jnp.where(low_b, sink_terms[0], sink_terms[1])
            ya_ref[rows, c * LANES:(c + 1) * LANES] = acc[:, :LANES] / denom

    o_ref[:, :D_ATTN] = _rms_rows(ya_ref[...], anorm_ref[...]).astype(o_ref.dtype)

    ri = lax.broadcasted_iota(jnp.int32, (RET_CHUNK, RET_CHUNK), 0)
    rj = lax.broadcasted_iota(jnp.int32, (RET_CHUNK, RET_CHUNK), 1)
    causal = ri >= rj
    for h in range(RET_HEADS):
        cols = slice(h * RET_HEAD_DIM, (h + 1) * RET_HEAD_DIM)
        pair, side = divmod(h, HEADS_PER_TILE)
        pcols = slice(side * RET_HEAD_DIM, (side + 1) * RET_HEAD_DIM)
        state = state_ref[h]
        for r0 in range(0, qa_ref.shape[0], RET_CHUNK):
            rows = slice(r0, r0 + RET_CHUNK)
            q = qr_refs[pair][rows, pcols]
            k = kr_refs[pair][rows, pcols]
            v = vr_refs[pair][rows, pcols]
            sc = lax.dot_general(q, k, qk_dims, preferred_element_type=F32)
            sc = jnp.where(causal, sc, 0.0).astype(BF16)
            y = (jnp.dot(sc, v, preferred_element_type=F32)
                 + jnp.dot(q, state.astype(BF16), preferred_element_type=F32))
            kv = lax.dot_general(k, v, (((0,), (0,)), ((), ())), preferred_element_type=F32)
            state = (state + kv) * cdec_ref[h]
            y = _rms_rows(y, rnorm_ref[:, cols])
            gate = gr_refs[pair][rows, pcols].astype(F32)
            o_ref[rows, D_ATTN + h * RET_HEAD_DIM:D_ATTN + (h + 1) * RET_HEAD_DIM] = (
                y * (gate * jax.nn.sigmoid(gate))).astype(o_ref.dtype)
        state_ref[h] = state


def _mixer(proj, sinks, cdec, anorm, rnorm, layer, batch):
    m = proj.shape[0]
    steps = m // batch // MIX_ROWS
    sub = MIX_ROWS // BLOCK

    def rows(width, col):
        return pl.BlockSpec((MIX_ROWS, width), lambda b, n: (b * steps + n, col))

    def prev_rows(width, col):
        return pl.BlockSpec(
            (BLOCK, width), lambda b, n: ((b * steps + n) * sub - jnp.minimum(n, 1), col))

    k_col = D_ATTN // D_KV
    ret_specs = [rows(TN_IN, RET_TILE0 + t) for t in range(4 * D_RET // TN_IN)]
    smem = pl.BlockSpec(memory_space=pltpu.SMEM)
    return pl.pallas_call(
        _mixer_kernel,
        grid=(batch, steps),
        in_specs=[rows(D_ATTN, 0),
                  rows(D_KV, k_col), prev_rows(D_KV, k_col),
                  rows(D_KV, k_col + 1), prev_rows(D_KV, k_col + 1),
                  *ret_specs,
                  smem, smem,
                  pl.BlockSpec((None, 1, D_ATTN), lambda b, n: (layer, 0, 0)),
                  pl.BlockSpec((None, 1, D_RET), lambda b, n: (layer, 0, 0))],
        out_specs=rows(D_ATTN + D_RET, 0),
        out_shape=jax.ShapeDtypeStruct((m, D_ATTN + D_RET), BF16),
        scratch_shapes=[pltpu.VMEM((RET_HEADS, RET_HEAD_DIM, RET_HEAD_DIM), F32),
                        pltpu.VMEM((MIX_ROWS, D_ATTN), F32)],
        compiler_params=pltpu.CompilerParams(
            dimension_semantics=("parallel", "arbitrary"),
            vmem_limit_bytes=VMEM_LIMIT),
        name="token_mixer",
    )(*([proj] * (5 + len(ret_specs))), sinks, cdec, anorm, rnorm)


def _weight_chunk_rows(k, n):
    rows = BF16_ROWS
    while rows * 2 * n * 4 <= W_CHUNK_BYTES and k % (rows * 2) == 0:
        rows *= 2
    return rows


def _load_weight_chunk(step, w_ref, wb_ref):
    ck = w_ref.shape[0]
    rows = pl.ds(pl.multiple_of(step * ck, ck), ck)
    wb_ref[rows, :] = w_ref[...].astype(wb_ref.dtype)


def _matmul_norm_res_kernel(n_load, a_ref, w_ref, x_ref, g_ref, o_ref, wb_ref):
    s = pl.program_id(0)

    @pl.when(s < n_load)
    def _():
        _load_weight_chunk(s, w_ref, wb_ref)

    @pl.when(s >= n_load)
    def _():
        y = jnp.dot(a_ref[...], wb_ref[...], preferred_element_type=F32)
        o_ref[...] = x_ref[...] + _rms_rows(y, g_ref[...])


def _matmul_norm_res(a, w, x, gain, layer):
    m, k = a.shape
    d = w.shape[2]
    chunk = _weight_chunk_rows(k, d)
    n_load = k // chunk

    def tile(s):
        return jnp.maximum(s - n_load, 0), 0

    return pl.pallas_call(
        functools.partial(_matmul_norm_res_kernel, n_load),
        grid=(n_load + m // TM_OUT,),
        in_specs=[pl.BlockSpec((TM_OUT, k), tile),
                  pl.BlockSpec((None, chunk, d),
                               lambda s: (layer, jnp.minimum(s, n_load - 1), 0)),
                  pl.BlockSpec((TM_OUT, d), tile),
                  pl.BlockSpec((None, 1, d), lambda s: (layer, 0, 0))],
        out_specs=pl.BlockSpec((TM_OUT, d), tile),
        out_shape=jax.ShapeDtypeStruct((m, d), F32),
        scratch_shapes=[pltpu.VMEM((k, d), BF16)],
        compiler_params=pltpu.CompilerParams(
            dimension_semantics=("arbitrary",), vmem_limit_bytes=VMEM_LIMIT),
        name="proj_norm_residual",
    )(a, w, x, gain)


def _ffn_up_kernel(tiles_per_seq, x_ref, g_ref, wa_ref, wg_ref, cwa_ref, cwg_ref,
                   cba_ref, cbg_ref, o_ref, h_ref, ua_ref, ug_ref, carry_ref):
    i = pl.program_id(0)
    j = pl.program_id(1)
    tm = x_ref.shape[0]

    @pl.when(j == 0)
    def _():
        _norm_into(x_ref, g_ref, h_ref)

    @pl.when((i == 0) & (j == 0))
    def _():
        carry_ref[...] = jnp.zeros_like(carry_ref)

    starts_sequence = (i % tiles_per_seq) == 0
    h = h_ref[...]

    def conv(u_ref, w_ref, cw, cb, slot):
        u = jnp.dot(h, w_ref[...].astype(BF16), preferred_element_type=F32)
        u_ref[0:SUBLANES, :] = jnp.where(starts_sequence, 0.0, carry_ref[slot, j])
        u_ref[SUBLANES:SUBLANES + tm, :] = u
        carry_ref[slot, j] = u[tm - SUBLANES:, :]
        return (cb + u_ref[SUBLANES - 2:SUBLANES - 2 + tm, :] * cw[0:1]
                + u_ref[SUBLANES - 1:SUBLANES - 1 + tm, :] * cw[1:2] + u * cw[2:3])

    a = conv(ua_ref, wa_ref, cwa_ref[...], cba_ref[...], 0)
    half_gate = conv(ug_ref, wg_ref, cwg_ref[...] * 0.5, cbg_ref[...] * 0.5, 1)
    t = jnp.tanh(a * (GELU_C1 + GELU_C2 * (a * a)))
    o_ref[...] = ((a + a * t) * half_gate).astype(o_ref.dtype)


def _ffn_up(x, gain, w_up, conv_w, conv_b, layer, seq):
    m, d = x.shape
    nj = D_FF // TN_IN
    kern = functools.partial(_ffn_up_kernel, seq // TM_IN)
    return pl.pallas_call(
        kern,
        grid=(m // TM_IN, nj),
        in_specs=[pl.BlockSpec((TM_IN, d), lambda i, j: (i, 0)),
                  pl.BlockSpec((None, 1, d), lambda i, j: (layer, 0, 0)),
                  pl.BlockSpec((None, d, TN_IN), lambda i, j: (layer, 0, j)),
                  pl.BlockSpec((None, d, TN_IN), lambda i, j: (layer, 0, j + nj)),
                  pl.BlockSpec((None, CONV_W, TN_IN), lambda i, j: (layer, 0, j)),
                  pl.BlockSpec((None, CONV_W, TN_IN), lambda i, j: (layer, 0, j + nj)),
                  pl.BlockSpec((None, 1, TN_IN), lambda i, j: (layer, 0, j)),
                  pl.BlockSpec((None, 1, TN_IN), lambda i, j: (layer, 0, j + nj))],
        out_specs=pl.BlockSpec((TM_IN, TN_IN), lambda i, j: (i, j)),
        out_shape=jax.ShapeDtypeStruct((m, D_FF), BF16),
        scratch_shapes=[pltpu.VMEM((TM_IN, d), BF16),
                        pltpu.VMEM((TM_IN + SUBLANES, TN_IN), F32),
                        pltpu.VMEM((TM_IN + SUBLANES, TN_IN), F32),
                        pltpu.VMEM((2, nj, SUBLANES, TN_IN), F32)],
        compiler_params=pltpu.CompilerParams(
            dimension_semantics=("arbitrary", "arbitrary"),
            vmem_limit_bytes=VMEM_LIMIT),
        name="ffn_up_conv_gate",
    )(x, gain, w_up, w_up, conv_w, conv_w, conv_b, conv_b)


def _retention_decay_tables():
    lg = jnp.log(1.0 - jnp.power(2.0, -5.0 - jnp.arange(RET_HEADS, dtype=F32)))
    idx = jnp.arange(RET_CHUNK, dtype=F32)
    q_dec = jnp.exp(lg[:, None] * idx)
    k_dec = jnp.exp(-lg[:, None] * idx) * RET_HEAD_DIM ** -0.5
    dec = jnp.concatenate([q_dec, k_dec], axis=0)[..., None]
    dec = jnp.broadcast_to(dec, (2 * RET_HEADS, RET_CHUNK, LANES)).astype(F32)
    chunk_dec = jnp.exp(lg * RET_CHUNK).astype(F32)
    return dec, chunk_dec


def kernel(x, positions, w_in, w_out, w_up, w_down, conv_w, conv_b, attn_sinks,
           pre_mix_norm, post_mix_norm, attn_out_norm, ret_out_norm,
           pre_ffn_norm, post_ffn_norm):
    batch, seq, d = x.shape
    m = batch * seq
    depth = w_in.shape[0]
    xf = x.reshape(m, d)
    pos_b = jnp.broadcast_to(positions.reshape(m, 1).astype(F32), (m, LANES))
    tables = _rope_tables(pos_b)
    dec, cdec = _retention_decay_tables()
    pre_mix = pre_mix_norm[:, None]
    post_mix = post_mix_norm[:, None]
    pre_ffn = pre_ffn_norm[:, None]
    post_ffn = post_ffn_norm[:, None]
    anorm = attn_out_norm[:, None]
    rnorm = ret_out_norm.reshape(depth, 1, D_RET)
    conv_b3 = conv_b[:, None]
    for l in range(depth):
        proj = _in_proj(xf, pre_mix, w_in, l, tables, dec)
        mix = _mixer(proj, attn_sinks[l], cdec, anorm, rnorm, l, batch)
        xf = _matmul_norm_res(mix, w_out, xf, post_mix, l)
        f = _ffn_up(xf, pre_ffn, w_up, conv_w, conv_b3, l, seq)
        xf = _matmul_norm_res(f, w_down, xf, post_ffn, l)
    return xf.reshape(batch, seq, d)
```

```python
import functools

import jax
import jax.numpy as jnp
from jax import lax
from jax.experimental import pallas as pl
from jax.experimental.pallas import tpu as pltpu

F32 = jnp.float32
BF16 = jnp.bfloat16

D_MODEL = 2048
D_ATTN = 1024
HEAD_DIM = 64
N_Q_HEADS = 16
N_KV_HEADS = 4
Q_PER_KV = N_Q_HEADS // N_KV_HEADS
BLOCK = 128
ROPE_DIM = 16
ROPE_THETA = 500000.0
D_RET = 1024
RET_HEADS = 4
RET_HEAD_DIM = 256
RET_THETA = 10000.0
D_KV = N_KV_HEADS * HEAD_DIM
D_IN = D_ATTN + 2 * D_KV + 4 * D_RET
D_FF = 5632
CONV_W = 3
EPS = 1e-6
NEG_INF = -1e30
LOG2E = 1.4426950408889634

LANES = 128
SUBLANES = 8
BF16_ROWS = 16
W_CHUNK_BYTES = 4 * 1024 * 1024
VMEM_LIMIT = 56 * 1024 * 1024
VMEM_BUDGET = 44 * 1024 * 1024

TM_IN = 1024
TN_IN = 512
TM_OUT = 256
NORM_ROWS = 128
MIX_ROWS = 512
RET_CHUNK = 256
assert MIX_ROWS % RET_CHUNK == 0 and TM_OUT % RET_CHUNK == 0
GELU_C1 = 0.7978845608028654
GELU_C2 = GELU_C1 * 0.044715

Q_TILES = D_ATTN // TN_IN
KV_TILE = Q_TILES
RET_TILE0 = (D_ATTN + 2 * D_KV) // TN_IN
ROPE_R_TILES = (RET_TILE0, RET_TILE0 + 2 * D_RET // TN_IN)
HEADS_PER_TILE = TN_IN // RET_HEAD_DIM
assert 2 * D_KV == TN_IN and (D_ATTN + 2 * D_KV) % TN_IN == 0


def _rms_rows(xf, gain):
    ms = jnp.mean(xf * xf, axis=-1, keepdims=True)
    return xf * lax.rsqrt(ms + EPS) * gain


def _norm_into(x_ref, g_ref, h_ref):
    gain = g_ref[...]

    def body(r, carry):
        rows = pl.ds(pl.multiple_of(r * NORM_ROWS, NORM_ROWS), NORM_ROWS)
        h_ref[rows, :] = _rms_rows(x_ref[rows, :], gain).astype(h_ref.dtype)
        return carry

    lax.fori_loop(0, x_ref.shape[0] // NORM_ROWS, body, 0)


def _rope_table_kernel(pos_ref, inv_r_ref, inv_a_ref, rot_ref, sa_ref, sb_ref,
                       rc_ref, rs_ref, ac_ref, asa_ref, asb_ref):
    pos = pos_ref[...]
    ang_r = pos * inv_r_ref[...]
    rc_ref[...] = jnp.cos(ang_r)
    rs_ref[...] = jnp.sin(ang_r)
    ang_a = pos * inv_a_ref[...]
    ca = jnp.cos(ang_a)
    sa = jnp.sin(ang_a)
    ac_ref[...] = jnp.where(rot_ref[...] > 0.5, ca, 1.0)
    asa_ref[...] = sa * sa_ref[...]
    asb_ref[...] = sa * sb_ref[...]


def _rope_tables(pos_b):
    m = pos_b.shape[0]
    half_r = RET_HEAD_DIM // 2
    inv_r = jnp.power(F32(RET_THETA), -jnp.arange(half_r, dtype=F32) / half_r)
    half_a = ROPE_DIM // 2
    inv_a8 = jnp.power(F32(ROPE_THETA), -jnp.arange(half_a, dtype=F32) / half_a)
    lane = jnp.arange(LANES) % HEAD_DIM
    inv_a = jnp.where(lane < ROPE_DIM, inv_a8[lane % half_a], 0.0).astype(F32)
    rot = (lane < ROPE_DIM).astype(F32)
    sa = jnp.where(lane < half_a, -1.0, 0.0).astype(F32)
    sb = jnp.where((lane >= half_a) & (lane < ROPE_DIM), 1.0, 0.0).astype(F32)
    rows = 2048
    row_spec = pl.BlockSpec((rows, LANES), lambda i: (i, 0))
    lane_spec = pl.BlockSpec((1, LANES), lambda i: (0, 0))
    return pl.pallas_call(
        _rope_table_kernel,
        grid=(m // rows,),
        in_specs=[row_spec] + [lane_spec] * 5,
        out_specs=[row_spec] * 5,
        out_shape=[jax.ShapeDtypeStruct((m, LANES), F32)] * 5,
        compiler_params=pltpu.CompilerParams(
            dimension_semantics=("parallel",), vmem_limit_bytes=VMEM_LIMIT),
        name="rope_tables",
    )(pos_b, inv_r[None], inv_a[None], rot[None], sa[None], sb[None])


def _in_proj_kernel(n_load, x_ref, g_ref, w_ref, ac_ref, asa_ref, asb_ref, rc_ref, rs_ref,
                    dec_ref, o_ref, wb_ref):
    s = pl.program_id(0)

    @pl.when(s < n_load)
    def _():
        _load_weight_chunk(s, w_ref, wb_ref)

    @pl.when(s >= n_load)
    def _():
        _in_proj_tile(x_ref, g_ref, wb_ref, ac_ref, asa_ref, asb_ref, rc_ref, rs_ref, dec_ref, o_ref)


def _in_proj_tile(x_ref, g_ref, w_ref, ac_ref, asa_ref, asb_ref, rc_ref, rs_ref, dec_ref, o_ref):
    tm = x_ref.shape[0]
    h = _rms_rows(x_ref[...], g_ref[...]).astype(BF16)

    def attn_rope(acc, base, n_cols, scale):
        cos, sin_a, sin_b = ac_ref[...], asa_ref[...], asb_ref[...]
        for c in range(n_cols):
            xc = acc[:, c * LANES:(c + 1) * LANES]
            y = (xc * cos + pltpu.roll(xc, LANES - ROPE_DIM // 2, 1) * sin_a
                 + pltpu.roll(xc, ROPE_DIM // 2, 1) * sin_b)
            if scale is not None:
                y = y * scale
            o_ref[:, base + c * LANES:base + (c + 1) * LANES] = y.astype(o_ref.dtype)

    for t in range(w_ref.shape[1] // TN_IN):
        base = t * TN_IN
        acc = jnp.dot(h, w_ref[:, base:base + TN_IN], preferred_element_type=F32)
        if t < Q_TILES:
            attn_rope(acc, base, TN_IN // LANES, HEAD_DIM ** -0.5 * LOG2E)
        elif t == KV_TILE:
            attn_rope(acc, base, D_KV // LANES, None)
            o_ref[:, base + D_KV:base + TN_IN] = acc[:, D_KV:].astype(o_ref.dtype)
        elif t < ROPE_R_TILES[1]:
            half = RET_HEAD_DIM // 2
            cos, sin = rc_ref[...], rs_ref[...]
            for hh in range(HEADS_PER_TILE):
                head = (t - ROPE_R_TILES[0]) * HEADS_PER_TILE + hh
                dec = jnp.concatenate([dec_ref[head]] * (tm // RET_CHUNK), axis=0)
                lo = hh * RET_HEAD_DIM
                x1 = acc[:, lo:lo + half]
                x2 = acc[:, lo + half:lo + 2 * half]
                o_ref[:, base + lo:base + lo + half] = (
                    (x1 * cos - x2 * sin) * dec).astype(o_ref.dtype)
                o_ref[:, base + lo + half:base + lo + 2 * half] = (
                    (x2 * cos + x1 * sin) * dec).astype(o_ref.dtype)
        else:
            o_ref[:, base:base + TN_IN] = acc.astype(o_ref.dtype)


def _in_proj(x, gain, w, layer, tables, dec):
    m, d = x.shape
    n = w.shape[2]
    rc, rs, ac, asa, asb = tables
    chunk = _weight_chunk_rows(d, n)
    n_load = d // chunk

    def tile(s):
        return jnp.maximum(s - n_load, 0), 0

    tab_spec = pl.BlockSpec((TM_OUT, LANES), tile)
    return pl.pallas_call(
        functools.partial(_in_proj_kernel, n_load),
        grid=(n_load + m // TM_OUT,),
        in_specs=[pl.BlockSpec((TM_OUT, d), tile),
                  pl.BlockSpec((None, 1, d), lambda s: (layer, 0, 0)),
                  pl.BlockSpec((None, chunk, n),
                               lambda s: (layer, jnp.minimum(s, n_load - 1), 0)),
                  tab_spec, tab_spec, tab_spec, tab_spec, tab_spec,
                  pl.BlockSpec(dec.shape, lambda s: (0, 0, 0))],
        out_specs=pl.BlockSpec((TM_OUT, n), tile),
        out_shape=jax.ShapeDtypeStruct((m, n), BF16),
        scratch_shapes=[pltpu.VMEM((d, n), BF16)],
        compiler_params=pltpu.CompilerParams(
            dimension_semantics=("arbitrary",), vmem_limit_bytes=VMEM_LIMIT),
        name="norm_in_proj_rope",
    )(x, gain, w, ac, asa, asb, rc, rs, dec)


def _mixer_kernel(qa_ref, kc_ref, kp_ref, vc_ref, vp_ref,
                  qr0_ref, qr1_ref, kr0_ref, kr1_ref, vr0_ref, vr1_ref, gr0_ref, gr1_ref,
                  sink_ref, cdec_ref, anorm_ref, rnorm_ref,
                  o_ref, state_ref, ya_ref):
    n = pl.program_id(1)
    qr_refs, kr_refs = (qr0_ref, qr1_ref), (kr0_ref, kr1_ref)
    vr_refs, gr_refs = (vr0_ref, vr1_ref), (gr0_ref, gr1_ref)

    @pl.when(n == 0)
    def _():
        state_ref[...] = jnp.zeros_like(state_ref)

    lane_b = lax.broadcasted_iota(jnp.int32, (BLOCK, LANES), 1)
    low_b = lane_b < HEAD_DIM
    ones_lo = jnp.where(low_b, 1.0, 0.0).astype(BF16)
    ones_hi = jnp.where(low_b, 0.0, 1.0).astype(BF16)

    def placed(x_b):
        x = x_b.astype(F32)
        xr = pltpu.roll(x, HEAD_DIM, 1)
        views = ((jnp.where(low_b, x, 0.0), jnp.where(low_b, 0.0, xr)),
                 (jnp.where(low_b, xr, 0.0), jnp.where(low_b, 0.0, x)))
        return [tuple(t.astype(BF16) for t in pair) for pair in views]

    def kv_matrices(k_ref, v_ref, rows):
        ks, vs = [], []
        for c in range(D_KV // LANES):
            cols = slice(c * LANES, (c + 1) * LANES)
            for k2, (lo, hi) in zip(placed(k_ref[rows, cols]), placed(v_ref[rows, cols])):
                ks.append(jnp.concatenate(k2, axis=0))
                vs.append(jnp.concatenate([jnp.concatenate([lo, ones_lo], axis=1),
                                           jnp.concatenate([hi, ones_hi], axis=1)], axis=0))
        return ks, vs

    n_sub = qa_ref.shape[0] // BLOCK
    blocks = [kv_matrices(kp_ref, vp_ref, slice(0, BLOCK))]
    blocks += [kv_matrices(kc_ref, vc_ref, slice(sb * BLOCK, (sb + 1) * BLOCK))
               for sb in range(n_sub)]

    qi = lax.broadcasted_iota(jnp.int32, (BLOCK, 2 * BLOCK), 0)
    kj = lax.broadcasted_iota(jnp.int32, (BLOCK, 2 * BLOCK), 1) & (BLOCK - 1)
    in_cur = kj <= qi
    no_prev = jnp.where(n > 0, 0.0, NEG_INF)
    qk_dims = (((1,), (1,)), ((), ()))

    for sb in range(n_sub):
        rows = slice(sb * BLOCK, (sb + 1) * BLOCK)
        (k_prev, v_prev), (k_cur, v_cur) = blocks[sb], blocks[sb + 1]
        for c in range(D_ATTN // LANES):
            g = (2 * c) // Q_PER_KV
            qc = qa_ref[rows, c * LANES:(c + 1) * LANES]
            s_cur = lax.dot_general(qc, k_cur[g], qk_dims, preferred_element_type=F32)
            s_prev = lax.dot_general(qc, k_prev[g], qk_dims, preferred_element_type=F32)
            if sb == 0:
                s_prev = s_prev + no_prev
            s = jnp.where(in_cur, s_cur, s_prev)
            ps, sink_terms = [], []
            for hh in range(2):
                sh = s[:, hh * BLOCK:(hh + 1) * BLOCK]
                sink = sink_ref[2 * c + hh] * LOG2E
                mx = jnp.maximum(jnp.max(sh, axis=-1, keepdims=True), sink)
                ps.append(jnp.exp2(sh - mx))
                sink_terms.append(jnp.exp2(sink - mx))
            p = jnp.concatenate(ps, axis=1)
            acc = (jnp.dot(jnp.where(in_cur, p, 0.0).astype(BF16), v_cur[g],
                           preferred_element_type=F32)
                   + jnp.dot(jnp.where(in_cur, 0.0, p).astype(BF16), v_prev[g],
                             preferred_element_type=F32))
            denom = acc[:, LANES:] + jnp.where(low_b, sink_terms[0], sink_terms[1])
            ya_ref[rows, c * LANES:(c + 1) * LANES] = acc[:, :LANES] / denom

    o_ref[:, :D_ATTN] = _rms_rows(ya_ref[...], anorm_ref[...]).astype(o_ref.dtype)

    ri = lax.broadcasted_iota(jnp.int32, (RET_CHUNK, RET_CHUNK), 0)
    rj = lax.broadcasted_iota(jnp.int32, (RET_CHUNK, RET_CHUNK), 1)
    causal = ri >= rj
    for h in range(RET_HEADS):
        cols = slice(h * RET_HEAD_DIM, (h + 1) * RET_HEAD_DIM)
        pair, side = divmod(h, HEADS_PER_TILE)
        pcols = slice(side * RET_HEAD_DIM, (side + 1) * RET_HEAD_DIM)
        state = state_ref[h]
        for r0 in range(0, qa_ref.shape[0], RET_CHUNK):
            rows = slice(r0, r0 + RET_CHUNK)
            q = qr_refs[pair][rows, pcols]
            k = kr_refs[pair][rows, pcols]
            v = vr_refs[pair][rows, pcols]
            sc = lax.dot_general(q, k, qk_dims, preferred_element_type=F32)
            sc = jnp.where(causal, sc, 0.0).astype(BF16)
            y = (jnp.dot(sc, v, preferred_element_type=F32)
                 + jnp.dot(q, state.astype(BF16), preferred_element_type=F32))
            kv = lax.dot_general(k, v, (((0,), (0,)), ((), ())), preferred_element_type=F32)
            state = (state + kv) * cdec_ref[h]
            y = _rms_rows(y, rnorm_ref[:, cols])
            gate = gr_refs[pair][rows, pcols].astype(F32)
            o_ref[rows, D_ATTN + h * RET_HEAD_DIM:D_ATTN + (h + 1) * RET_HEAD_DIM] = (
                y * (gate * jax.nn.sigmoid(gate))).astype(o_ref.dtype)
        state_ref[h] = state


def _mixer(proj, sinks, cdec, anorm, rnorm, layer, batch):
    m = proj.shape[0]
    steps = m // batch // MIX_ROWS
    sub = MIX_ROWS // BLOCK

    def rows(width, col):
        return pl.BlockSpec((MIX_ROWS, width), lambda b, n: (b * steps + n, col))

    def prev_rows(width, col):
        return pl.BlockSpec(
            (BLOCK, width), lambda b, n: ((b * steps + n) * sub - jnp.minimum(n, 1), col))

    k_col = D_ATTN // D_KV
    ret_specs = [rows(TN_IN, RET_TILE0 + t) for t in range(4 * D_RET // TN_IN)]
    smem = pl.BlockSpec(memory_space=pltpu.SMEM)
    return pl.pallas_call(
        _mixer_kernel,
        grid=(batch, steps),
        in_specs=[rows(D_ATTN, 0),
                  rows(D_KV, k_col), prev_rows(D_KV, k_col),
                  rows(D_KV, k_col + 1), prev_rows(D_KV, k_col + 1),
                  *ret_specs,
                  smem, smem,
                  pl.BlockSpec((None, 1, D_ATTN), lambda b, n: (layer, 0, 0)),
                  pl.BlockSpec((None, 1, D_RET), lambda b, n: (layer, 0, 0))],
        out_specs=rows(D_ATTN + D_RET, 0),
        out_shape=jax.ShapeDtypeStruct((m, D_ATTN + D_RET), BF16),
        scratch_shapes=[pltpu.VMEM((RET_HEADS, RET_HEAD_DIM, RET_HEAD_DIM), F32),
                        pltpu.VMEM((MIX_ROWS, D_ATTN), F32)],
        compiler_params=pltpu.CompilerParams(
            dimension_semantics=("parallel", "arbitrary"),
            vmem_limit_bytes=VMEM_LIMIT),
        name="token_mixer",
    )(*([proj] * (5 + len(ret_specs))), sinks, cdec, anorm, rnorm)


def _weight_chunk_rows(k, n):
    rows = BF16_ROWS
    while rows * 2 * n * 4 <= W_CHUNK_BYTES and k % (rows * 2) == 0:
        rows *= 2
    return rows


def _load_weight_chunk(step, w_ref, wb_ref):
    ck = w_ref.shape[0]
    rows = pl.ds(pl.multiple_of(step * ck, ck), ck)
    wb_ref[rows, :] = w_ref[...].astype(wb_ref.dtype)


def _matmul_norm_res_kernel(n_load, a_ref, w_ref, x_ref, g_ref, o_ref, wb_ref):
    s = pl.program_id(0)

    @pl.when(s < n_load)
    def _():
        _load_weight_chunk(s, w_ref, wb_ref)

    @pl.when(s >= n_load)
    def _():
        for r0 in range(0, a_ref.shape[0], TM_OUT):
            rows = slice(r0, r0 + TM_OUT)
            y = jnp.dot(a_ref[rows, :], wb_ref[...], preferred_element_type=F32)
            o_ref[rows, :] = x_ref[rows, :] + _rms_rows(y, g_ref[...])


def _matmul_norm_res(a, w, x, gain, layer):
    m, k = a.shape
    d = w.shape[2]
    chunk = _weight_chunk_rows(k, d)
    n_load = k // chunk
    resident = k * d * 2 + 2 * chunk * d * 4
    per_row = 2 * (k * 2 + d * 4 + d * 4)
    tm = TM_OUT
    while resident + 2 * tm * per_row <= VMEM_BUDGET and m % (2 * tm) == 0 and tm < TM_IN:
        tm *= 2

    def tile(s):
        return jnp.maximum(s - n_load, 0), 0

    return pl.pallas_call(
        functools.partial(_matmul_norm_res_kernel, n_load),
        grid=(n_load + m // tm,),
        in_specs=[pl.BlockSpec((tm, k), tile),
                  pl.BlockSpec((None, chunk, d),
                               lambda s: (layer, jnp.minimum(s, n_load - 1), 0)),
                  pl.BlockSpec((tm, d), tile),
                  pl.BlockSpec((None, 1, d), lambda s: (layer, 0, 0))],
        out_specs=pl.BlockSpec((tm, d), tile),
        out_shape=jax.ShapeDtypeStruct((m, d), F32),
        scratch_shapes=[pltpu.VMEM((k, d), BF16)],
        compiler_params=pltpu.CompilerParams(
            dimension_semantics=("arbitrary",), vmem_limit_bytes=VMEM_LIMIT),
        name="proj_norm_residual",
    )(a, w, x, gain)


def _ffn_up_kernel(tiles_per_seq, x_ref, g_ref, wa_ref, wg_ref, cwa_ref, cwg_ref,
                   cba_ref, cbg_ref, o_ref, h_ref, ua_ref, ug_ref, carry_ref):
    i = pl.program_id(0)
    j = pl.program_id(1)
    tm = x_ref.shape[0]

    @pl.when(j == 0)
    def _():
        _norm_into(x_ref, g_ref, h_ref)

    @pl.when((i == 0) & (j == 0))
    def _():
        carry_ref[...] = jnp.zeros_like(carry_ref)

    starts_sequence = (i % tiles_per_seq) == 0
    h = h_ref[...]

    def conv(u_ref, w_ref, cw, cb, slot):
        u = jnp.dot(h, w_ref[...].astype(BF16), preferred_element_type=F32)
        u_ref[0:SUBLANES, :] = jnp.where(starts_sequence, 0.0, carry_ref[slot, j])
        u_ref[SUBLANES:SUBLANES + tm, :] = u
        carry_ref[slot, j] = u[tm - SUBLANES:, :]
        return (cb + u_ref[SUBLANES - 2:SUBLANES - 2 + tm, :] * cw[0:1]
                + u_ref[SUBLANES - 1:SUBLANES - 1 + tm, :] * cw[1:2] + u * cw[2:3])

    a = conv(ua_ref, wa_ref, cwa_ref[...], cba_ref[...], 0)
    half_gate = conv(ug_ref, wg_ref, cwg_ref[...] * 0.5, cbg_ref[...] * 0.5, 1)
    t = jnp.tanh(a * (GELU_C1 + GELU_C2 * (a * a)))
    o_ref[...] = ((a + a * t) * half_gate).astype(o_ref.dtype)


def _ffn_up(x, gain, w_up, conv_w, conv_b, layer, seq):
    m, d = x.shape
    nj = D_FF // TN_IN
    kern = functools.partial(_ffn_up_kernel, seq // TM_IN)
    return pl.pallas_call(
        kern,
        grid=(m // TM_IN, nj),
        in_specs=[pl.BlockSpec((TM_IN, d), lambda i, j: (i, 0)),
                  pl.BlockSpec((None, 1, d), lambda i, j: (layer, 0, 0)),
                  pl.BlockSpec((None, d, TN_IN), lambda i, j: (layer, 0, j)),
                  pl.BlockSpec((None, d, TN_IN), lambda i, j: (layer, 0, j + nj)),
                  pl.BlockSpec((None, CONV_W, TN_IN), lambda i, j: (layer, 0, j)),
                  pl.BlockSpec((None, CONV_W, TN_IN), lambda i, j: (layer, 0, j + nj)),
                  pl.BlockSpec((None, 1, TN_IN), lambda i, j: (layer, 0, j)),
                  pl.BlockSpec((None, 1, TN_IN), lambda i, j: (layer, 0, j + nj))],
        out_specs=pl.BlockSpec((TM_IN, TN_IN), lambda i, j: (i, j)),
        out_shape=jax.ShapeDtypeStruct((m, D_FF), BF16),
        scratch_shapes=[pltpu.VMEM((TM_IN, d), BF16),
                        pltpu.VMEM((TM_IN + SUBLANES, TN_IN), F32),
                        pltpu.VMEM((TM_IN + SUBLANES, TN_IN), F32),
                        pltpu.VMEM((2, nj, SUBLANES, TN_IN), F32)],
        compiler_params=pltpu.CompilerParams(
            dimension_semantics=("arbitrary", "arbitrary"),
            vmem_limit_bytes=VMEM_LIMIT),
        name="ffn_up_conv_gate",
    )(x, gain, w_up, w_up, conv_w, conv_w, conv_b, conv_b)


def _retention_decay_tables():
    lg = jnp.log(1.0 - jnp.power(2.0, -5.0 - jnp.arange(RET_HEADS, dtype=F32)))
    idx = jnp.arange(RET_CHUNK, dtype=F32)
    q_dec = jnp.exp(lg[:, None] * idx)
    k_dec = jnp.exp(-lg[:, None] * idx) * RET_HEAD_DIM ** -0.5
    dec = jnp.concatenate([q_dec, k_dec], axis=0)[..., None]
    dec = jnp.broadcast_to(dec, (2 * RET_HEADS, RET_CHUNK, LANES)).astype(F32)
    chunk_dec = jnp.exp(lg * RET_CHUNK).astype(F32)
    return dec, chunk_dec


def kernel(x, positions, w_in, w_out, w_up, w_down, conv_w, conv_b, attn_sinks,
           pre_mix_norm, post_mix_norm, attn_out_norm, ret_out_norm,
           pre_ffn_norm, post_ffn_norm):
    batch, seq, d = x.shape
    m = batch * seq
    depth = w_in.shape[0]
    xf = x.reshape(m, d)
    pos_b = jnp.broadcast_to(positions.reshape(m, 1).astype(F32), (m, LANES))
    tables = _rope_tables(pos_b)
    dec, cdec = _retention_decay_tables()
    pre_mix = pre_mix_norm[:, None]
    post_mix = post_mix_norm[:, None]
    pre_ffn = pre_ffn_norm[:, None]
    post_ffn = post_ffn_norm[:, None]
    anorm = attn_out_norm[:, None]
    rnorm = ret_out_norm.reshape(depth, 1, D_RET)
    conv_b3 = conv_b[:, None]
    for l in range(depth):
        proj = _in_proj(xf, pre_mix, w_in, l, tables, dec)
        mix = _mixer(proj, attn_sinks[l], cdec, anorm, rnorm, l, batch)
        xf = _matmul_norm_res(mix, w_out, xf, post_mix, l)
        f = _ffn_up(xf, pre_ffn, w_up, conv_w, conv_b3, l, seq)
        xf = _matmul_norm_res(f, w_down, xf, post_ffn, l)
    return xf.reshape(batch, seq, d)
```

```python
import functools

import jax
import jax.numpy as jnp
from jax import lax
from jax.experimental import pallas as pl
from jax.experimental.pallas import tpu as pltpu

F32 = jnp.float32
BF16 = jnp.bfloat16

D_MODEL = 2048
D_ATTN = 1024
HEAD_DIM = 64
N_Q_HEADS = 16
N_KV_HEADS = 4
Q_PER_KV = N_Q_HEADS // N_KV_HEADS
BLOCK = 128
ROPE_DIM = 16
ROPE_THETA = 500000.0
D_RET = 1024
RET_HEADS = 4
RET_HEAD_DIM = 256
RET_THETA = 10000.0
D_KV = N_KV_HEADS * HEAD_DIM
D_IN = D_ATTN + 2 * D_KV + 4 * D_RET
D_FF = 5632
CONV_W = 3
EPS = 1e-6
NEG_INF = -1e30
LOG2E = 1.4426950408889634

LANES = 128
SUBLANES = 8
BF16_ROWS = 16
W_CHUNK_BYTES = 4 * 1024 * 1024
VMEM_LIMIT = 56 * 1024 * 1024
VMEM_BUDGET = 44 * 1024 * 1024

TM_IN = 1024
TN_IN = 512
TM_OUT = 256
NORM_ROWS = 128
MIX_ROWS = 1024
RET_CHUNK = 256
assert MIX_ROWS % RET_CHUNK == 0 and TM_OUT % RET_CHUNK == 0
GELU_C1 = 0.7978845608028654
GELU_C2 = GELU_C1 * 0.044715

Q_TILES = D_ATTN // TN_IN
KV_TILE = Q_TILES
RET_TILE0 = (D_ATTN + 2 * D_KV) // TN_IN
ROPE_R_TILES = (RET_TILE0, RET_TILE0 + 2 * D_RET // TN_IN)
HEADS_PER_TILE = TN_IN // RET_HEAD_DIM
assert 2 * D_KV == TN_IN and (D_ATTN + 2 * D_KV) % TN_IN == 0


def _rms_rows(xf, gain):
    ms = jnp.mean(xf * xf, axis=-1, keepdims=True)
    return xf * lax.rsqrt(ms + EPS) * gain


def _norm_into(x_ref, g_ref, h_ref):
    gain = g_ref[...]

    def body(r, carry):
        rows = pl.ds(pl.multiple_of(r * NORM_ROWS, NORM_ROWS), NORM_ROWS)
        h_ref[rows, :] = _rms_rows(x_ref[rows, :], gain).astype(h_ref.dtype)
        return carry

    lax.fori_loop(0, x_ref.shape[0] // NORM_ROWS, body, 0)


def _rope_table_kernel(pos_ref, inv_r_ref, inv_a_ref, rot_ref, sa_ref, sb_ref,
                       rc_ref, rs_ref, ac_ref, asa_ref, asb_ref):
    pos = pos_ref[...]
    ang_r = pos * inv_r_ref[...]
    rc_ref[...] = jnp.cos(ang_r)
    rs_ref[...] = jnp.sin(ang_r)
    ang_a = pos * inv_a_ref[...]
    ca = jnp.cos(ang_a)
    sa = jnp.sin(ang_a)
    ac_ref[...] = jnp.where(rot_ref[...] > 0.5, ca, 1.0)
    asa_ref[...] = sa * sa_ref[...]
    asb_ref[...] = sa * sb_ref[...]


def _rope_tables(pos_b):
    m = pos_b.shape[0]
    half_r = RET_HEAD_DIM // 2
    inv_r = jnp.power(F32(RET_THETA), -jnp.arange(half_r, dtype=F32) / half_r)
    half_a = ROPE_DIM // 2
    inv_a8 = jnp.power(F32(ROPE_THETA), -jnp.arange(half_a, dtype=F32) / half_a)
    lane = jnp.arange(LANES) % HEAD_DIM
    inv_a = jnp.where(lane < ROPE_DIM, inv_a8[lane % half_a], 0.0).astype(F32)
    rot = (lane < ROPE_DIM).astype(F32)
    sa = jnp.where(lane < half_a, -1.0, 0.0).astype(F32)
    sb = jnp.where((lane >= half_a) & (lane < ROPE_DIM), 1.0, 0.0).astype(F32)
    rows = 2048
    row_spec = pl.BlockSpec((rows, LANES), lambda i: (i, 0))
    lane_spec = pl.BlockSpec((1, LANES), lambda i: (0, 0))
    return pl.pallas_call(
        _rope_table_kernel,
        grid=(m // rows,),
        in_specs=[row_spec] + [lane_spec] * 5,
        out_specs=[row_spec] * 5,
        out_shape=[jax.ShapeDtypeStruct((m, LANES), F32)] * 5,
        compiler_params=pltpu.CompilerParams(
            dimension_semantics=("parallel",), vmem_limit_bytes=VMEM_LIMIT),
        name="rope_tables",
    )(pos_b, inv_r[None], inv_a[None], rot[None], sa[None], sb[None])


def _in_proj_kernel(n_load, x_ref, g_ref, w_ref, ac_ref, asa_ref, asb_ref, rc_ref, rs_ref,
                    dec_ref, o_ref, wb_ref):
    s = pl.program_id(0)

    @pl.when(s < n_load)
    def _():
        _load_weight_chunk(s, w_ref, wb_ref)

    @pl.when(s >= n_load)
    def _():
        _in_proj_tile(x_ref, g_ref, wb_ref, ac_ref, asa_ref, asb_ref, rc_ref, rs_ref, dec_ref, o_ref)


def _in_proj_tile(x_ref, g_ref, w_ref, ac_ref, asa_ref, asb_ref, rc_ref, rs_ref, dec_ref, o_ref):
    tm = x_ref.shape[0]
    h = _rms_rows(x_ref[...], g_ref[...]).astype(BF16)

    def attn_rope(acc, base, n_cols, scale):
        cos, sin_a, sin_b = ac_ref[...], asa_ref[...], asb_ref[...]
        for c in range(n_cols):
            xc = acc[:, c * LANES:(c + 1) * LANES]
            y = (xc * cos + pltpu.roll(xc, LANES - ROPE_DIM // 2, 1) * sin_a
                 + pltpu.roll(xc, ROPE_DIM // 2, 1) * sin_b)
            if scale is not None:
                y = y * scale
            o_ref[:, base + c * LANES:base + (c + 1) * LANES] = y.astype(o_ref.dtype)

    for t in range(w_ref.shape[1] // TN_IN):
        base = t * TN_IN
        acc = jnp.dot(h, w_ref[:, base:base + TN_IN], preferred_element_type=F32)
        if t < Q_TILES:
            attn_rope(acc, base, TN_IN // LANES, HEAD_DIM ** -0.5 * LOG2E)
        elif t == KV_TILE:
            attn_rope(acc, base, D_KV // LANES, None)
            o_ref[:, base + D_KV:base + TN_IN] = acc[:, D_KV:].astype(o_ref.dtype)
        elif t < ROPE_R_TILES[1]:
            half = RET_HEAD_DIM // 2
            cos, sin = rc_ref[...], rs_ref[...]
            for hh in range(HEADS_PER_TILE):
                head = (t - ROPE_R_TILES[0]) * HEADS_PER_TILE + hh
                dec = jnp.concatenate([dec_ref[head]] * (tm // RET_CHUNK), axis=0)
                lo = hh * RET_HEAD_DIM
                x1 = acc[:, lo:lo + half]
                x2 = acc[:, lo + half:lo + 2 * half]
                o_ref[:, base + lo:base + lo + half] = (
                    (x1 * cos - x2 * sin) * dec).astype(o_ref.dtype)
                o_ref[:, base + lo + half:base + lo + 2 * half] = (
                    (x2 * cos + x1 * sin) * dec).astype(o_ref.dtype)
        else:
            o_ref[:, base:base + TN_IN] = acc.astype(o_ref.dtype)


def _in_proj(x, gain, w, layer, tables, dec):
    m, d = x.shape
    n = w.shape[2]
    rc, rs, ac, asa, asb = tables
    chunk = _weight_chunk_rows(d, n)
    n_load = d // chunk

    def tile(s):
        return jnp.maximum(s - n_load, 0), 0

    tab_spec = pl.BlockSpec((TM_OUT, LANES), tile)
    return pl.pallas_call(
        functools.partial(_in_proj_kernel, n_load),
        grid=(n_load + m // TM_OUT,),
        in_specs=[pl.BlockSpec((TM_OUT, d), tile),
                  pl.BlockSpec((None, 1, d), lambda s: (layer, 0, 0)),
                  pl.BlockSpec((None, chunk, n),
                               lambda s: (layer, jnp.minimum(s, n_load - 1), 0)),
                  tab_spec, tab_spec, tab_spec, tab_spec, tab_spec,
                  pl.BlockSpec(dec.shape, lambda s: (0, 0, 0))],
        out_specs=pl.BlockSpec((TM_OUT, n), tile),
        out_shape=jax.ShapeDtypeStruct((m, n), BF16),
        scratch_shapes=[pltpu.VMEM((d, n), BF16)],
        compiler_params=pltpu.CompilerParams(
            dimension_semantics=("arbitrary",), vmem_limit_bytes=VMEM_LIMIT),
        name="norm_in_proj_rope",
    )(x, gain, w, ac, asa, asb, rc, rs, dec)


def _mixer_kernel(qa_ref, kc_ref, kp_ref, vc_ref, vp_ref,
                  qr0_ref, qr1_ref, kr0_ref, kr1_ref, vr0_ref, vr1_ref, gr0_ref, gr1_ref,
                  sink_ref, cdec_ref, anorm_ref, rnorm_ref,
                  o_ref, state_ref, ya_ref):
    n = pl.program_id(1)
    qr_refs, kr_refs = (qr0_ref, qr1_ref), (kr0_ref, kr1_ref)
    vr_refs, gr_refs = (vr0_ref, vr1_ref), (gr0_ref, gr1_ref)

    @pl.when(n == 0)
    def _():
        state_ref[...] = jnp.zeros_like(state_ref)

    lane_b = lax.broadcasted_iota(jnp.int32, (BLOCK, LANES), 1)
    low_b = lane_b < HEAD_DIM
    ones_lo = jnp.where(low_b, 1.0, 0.0).astype(BF16)
    ones_hi = jnp.where(low_b, 0.0, 1.0).astype(BF16)

    def placed(x_b):
        x = x_b.astype(F32)
        xr = pltpu.roll(x, HEAD_DIM, 1)
        views = ((jnp.where(low_b, x, 0.0), jnp.where(low_b, 0.0, xr)),
                 (jnp.where(low_b, xr, 0.0), jnp.where(low_b, 0.0, x)))
        return [tuple(t.astype(BF16) for t in pair) for pair in views]

    def kv_matrices(k_ref, v_ref, rows):
        ks, vs = [], []
        for c in range(D_KV // LANES):
            cols = slice(c * LANES, (c + 1) * LANES)
            for k2, (lo, hi) in zip(placed(k_ref[rows, cols]), placed(v_ref[rows, cols])):
                ks.append(jnp.concatenate(k2, axis=0))
                vs.append(jnp.concatenate([jnp.concatenate([lo, ones_lo], axis=1),
                                           jnp.concatenate([hi, ones_hi], axis=1)], axis=0))
        return ks, vs

    n_sub = qa_ref.shape[0] // BLOCK
    blocks = [kv_matrices(kp_ref, vp_ref, slice(0, BLOCK))]
    blocks += [kv_matrices(kc_ref, vc_ref, slice(sb * BLOCK, (sb + 1) * BLOCK))
               for sb in range(n_sub)]

    qi = lax.broadcasted_iota(jnp.int32, (BLOCK, 2 * BLOCK), 0)
    kj = lax.broadcasted_iota(jnp.int32, (BLOCK, 2 * BLOCK), 1) & (BLOCK - 1)
    in_cur = kj <= qi
    no_prev = jnp.where(n > 0, 0.0, NEG_INF)
    qk_dims = (((1,), (1,)), ((), ()))

    for sb in range(n_sub):
        rows = slice(sb * BLOCK, (sb + 1) * BLOCK)
        (k_prev, v_prev), (k_cur, v_cur) = blocks[sb], blocks[sb + 1]
        for c in range(D_ATTN // LANES):
            g = (2 * c) // Q_PER_KV
            qc = qa_ref[rows, c * LANES:(c + 1) * LANES]
            s_cur = lax.dot_general(qc, k_cur[g], qk_dims, preferred_element_type=F32)
            s_prev = lax.dot_general(qc, k_prev[g], qk_dims, preferred_element_type=F32)
            if sb == 0:
                s_prev = s_prev + no_prev
            s = jnp.where(in_cur, s_cur, s_prev)
            ps, sink_terms = [], []
            for hh in range(2):
                sh = s[:, hh * BLOCK:(hh + 1) * BLOCK]
                sink = sink_ref[2 * c + hh] * LOG2E
                mx = jnp.maximum(jnp.max(sh, axis=-1, keepdims=True), sink)
                ps.append(jnp.exp2(sh - mx))
                sink_terms.append(jnp.exp2(sink - mx))
            p = jnp.concatenate(ps, axis=1)
            acc = (jnp.dot(jnp.where(in_cur, p, 0.0).astype(BF16), v_cur[g],
                           preferred_element_type=F32)
                   + jnp.dot(jnp.where(in_cur, 0.0, p).astype(BF16), v_prev[g],
                             preferred_element_type=F32))
            denom = acc[:, LANES:] + jnp.where(low_b, sink_terms[0], sink_terms[1])
            ya_ref[rows, c * LANES:(c + 1) * LANES] = acc[:, :LANES] / denom

    o_ref[:, :D_ATTN] = _rms_rows(ya_ref[...], anorm_ref[...]).astype(o_ref.dtype)

    ri = lax.broadcasted_iota(jnp.int32, (RET_CHUNK, RET_CHUNK), 0)
    rj = lax.broadcasted_iota(jnp.int32, (RET_CHUNK, RET_CHUNK), 1)
    causal = ri >= rj
    for h in range(RET_HEADS):
        cols = slice(h * RET_HEAD_DIM, (h + 1) * RET_HEAD_DIM)
        pair, side = divmod(h, HEADS_PER_TILE)
        pcols = slice(side * RET_HEAD_DIM, (side + 1) * RET_HEAD_DIM)
        state = state_ref[h]
        for r0 in range(0, qa_ref.shape[0], RET_CHUNK):
            rows = slice(r0, r0 + RET_CHUNK)
            q = qr_refs[pair][rows, pcols]
            k = kr_refs[pair][rows, pcols]
            v = vr_refs[pair][rows, pcols]
            sc = lax.dot_general(q, k, qk_dims, preferred_element_type=F32)
            sc = jnp.where(causal, sc, 0.0).astype(BF16)
            y = (jnp.dot(sc, v, preferred_element_type=F32)
                 + jnp.dot(q, state.astype(BF16), preferred_element_type=F32))
            kv = lax.dot_general(k, v, (((0,), (0,)), ((), ())), preferred_element_type=F32)
            state = (state + kv) * cdec_ref[h]
            y = _rms_rows(y, rnorm_ref[:, cols])
            gate = gr_refs[pair][rows, pcols].astype(F32)
            o_ref[rows, D_ATTN + h * RET_HEAD_DIM:D_ATTN + (h + 1) * RET_HEAD_DIM] = (
                y * (gate * jax.nn.sigmoid(gate))).astype(o_ref.dtype)
        state_ref[h] = state


def _mixer(proj, sinks, cdec, anorm, rnorm, layer, batch):
    m = proj.shape[0]
    steps = m // batch // MIX_ROWS
    sub = MIX_ROWS // BLOCK

    def rows(width, col):
        return pl.BlockSpec((MIX_ROWS, width), lambda b, n: (b * steps + n, col))

    def prev_rows(width, col):
        return pl.BlockSpec(
            (BLOCK, width), lambda b, n: ((b * steps + n) * sub - jnp.minimum(n, 1), col))

    k_col = D_ATTN // D_KV
    ret_specs = [rows(TN_IN, RET_TILE0 + t) for t in range(4 * D_RET // TN_IN)]
    smem = pl.BlockSpec(memory_space=pltpu.SMEM)
    return pl.pallas_call(
        _mixer_kernel,
        grid=(batch, steps),
        in_specs=[rows(D_ATTN, 0),
                  rows(D_KV, k_col), prev_rows(D_KV, k_col),
                  rows(D_KV, k_col + 1), prev_rows(D_KV, k_col + 1),
                  *ret_specs,
                  smem, smem,
                  pl.BlockSpec((None, 1, D_ATTN), lambda b, n: (layer, 0, 0)),
                  pl.BlockSpec((None, 1, D_RET), lambda b, n: (layer, 0, 0))],
        out_specs=rows(D_ATTN + D_RET, 0),
        out_shape=jax.ShapeDtypeStruct((m, D_ATTN + D_RET), BF16),
        scratch_shapes=[pltpu.VMEM((RET_HEADS, RET_HEAD_DIM, RET_HEAD_DIM), F32),
                        pltpu.VMEM((MIX_ROWS, D_ATTN), F32)],
        compiler_params=pltpu.CompilerParams(
            dimension_semantics=("parallel", "arbitrary"),
            vmem_limit_bytes=VMEM_LIMIT),
        name="token_mixer",
    )(*([proj] * (5 + len(ret_specs))), sinks, cdec, anorm, rnorm)


def _weight_chunk_rows(k, n):
    rows = BF16_ROWS
    while rows * 2 * n * 4 <= W_CHUNK_BYTES and k % (rows * 2) == 0:
        rows *= 2
    return rows


def _load_weight_chunk(step, w_ref, wb_ref):
    ck = w_ref.shape[0]
    rows = pl.ds(pl.multiple_of(step * ck, ck), ck)
    wb_ref[rows, :] = w_ref[...].astype(wb_ref.dtype)


def _matmul_norm_res_kernel(n_load, a_ref, w_ref, x_ref, g_ref, o_ref, wb_ref):
    s = pl.program_id(0)

    @pl.when(s < n_load)
    def _():
        _load_weight_chunk(s, w_ref, wb_ref)

    @pl.when(s >= n_load)
    def _():
        for r0 in range(0, a_ref.shape[0], TM_OUT):
            rows = slice(r0, r0 + TM_OUT)
            y = jnp.dot(a_ref[rows, :], wb_ref[...], preferred_element_type=F32)
            o_ref[rows, :] = x_ref[rows, :] + _rms_rows(y, g_ref[...])


def _matmul_norm_res(a, w, x, gain, layer):
    m, k = a.shape
    d = w.shape[2]
    chunk = _weight_chunk_rows(k, d)
    n_load = k // chunk
    resident = k * d * 2 + 2 * chunk * d * 4
    per_row = 2 * (k * 2 + d * 4 + d * 4)
    tm = TM_OUT
    while resident + 2 * tm * per_row <= VMEM_BUDGET and m % (2 * tm) == 0 and tm < TM_IN:
        tm *= 2

    def tile(s):
        return jnp.maximum(s - n_load, 0), 0

    return pl.pallas_call(
        functools.partial(_matmul_norm_res_kernel, n_load),
        grid=(n_load + m // tm,),
        in_specs=[pl.BlockSpec((tm, k), tile),
                  pl.BlockSpec((None, chunk, d),
                               lambda s: (layer, jnp.minimum(s, n_load - 1), 0)),
                  pl.BlockSpec((tm, d), tile),
                  pl.BlockSpec((None, 1, d), lambda s: (layer, 0, 0))],
        out_specs=pl.BlockSpec((tm, d), tile),
        out_shape=jax.ShapeDtypeStruct((m, d), F32),
        scratch_shapes=[pltpu.VMEM((k, d), BF16)],
        compiler_params=pltpu.CompilerParams(
            dimension_semantics=("arbitrary",), vmem_limit_bytes=VMEM_LIMIT),
        name="proj_norm_residual",
    )(a, w, x, gain)


def _ffn_up_kernel(tiles_per_seq, x_ref, g_ref, wa_ref, wg_ref, cwa_ref, cwg_ref,
                   cba_ref, cbg_ref, o_ref, h_ref, ua_ref, ug_ref, carry_ref):
    i = pl.program_id(0)
    j = pl.program_id(1)
    tm = x_ref.shape[0]

    @pl.when(j == 0)
    def _():
        _norm_into(x_ref, g_ref, h_ref)

    @pl.when((i == 0) & (j == 0))
    def _():
        carry_ref[...] = jnp.zeros_like(carry_ref)

    starts_sequence = (i % tiles_per_seq) == 0
    h = h_ref[...]

    def conv(u_ref, w_ref, cw, cb, slot):
        u = jnp.dot(h, w_ref[...].astype(BF16), preferred_element_type=F32)
        u_ref[0:SUBLANES, :] = jnp.where(starts_sequence, 0.0, carry_ref[slot, j])
        u_ref[SUBLANES:SUBLANES + tm, :] = u
        carry_ref[slot, j] = u[tm - SUBLANES:, :]
        return (cb + u_ref[SUBLANES - 2:SUBLANES - 2 + tm, :] * cw[0:1]
                + u_ref[SUBLANES - 1:SUBLANES - 1 + tm, :] * cw[1:2] + u * cw[2:3])

    a = conv(ua_ref, wa_ref, cwa_ref[...], cba_ref[...], 0)
    half_gate = conv(ug_ref, wg_ref, cwg_ref[...] * 0.5, cbg_ref[...] * 0.5, 1)
    t = jnp.tanh(a * (GELU_C1 + GELU_C2 * (a * a)))
    o_ref[...] = ((a + a * t) * half_gate).astype(o_ref.dtype)


def _ffn_up(x, gain, w_up, conv_w, conv_b, layer, seq):
    m, d = x.shape
    nj = D_FF // TN_IN
    kern = functools.partial(_ffn_up_kernel, seq // TM_IN)
    return pl.pallas_call(
        kern,
        grid=(m // TM_IN, nj),
        in_specs=[pl.BlockSpec((TM_IN, d), lambda i, j: (i, 0)),
                  pl.BlockSpec((None, 1, d), lambda i, j: (layer, 0, 0)),
                  pl.BlockSpec((None, d, TN_IN), lambda i, j: (layer, 0, j)),
                  pl.BlockSpec((None, d, TN_IN), lambda i, j: (layer, 0, j + nj)),
                  pl.BlockSpec((None, CONV_W, TN_IN), lambda i, j: (layer, 0, j)),
                  pl.BlockSpec((None, CONV_W, TN_IN), lambda i, j: (layer, 0, j + nj)),
                  pl.BlockSpec((None, 1, TN_IN), lambda i, j: (layer, 0, j)),
                  pl.BlockSpec((None, 1, TN_IN), lambda i, j: (layer, 0, j + nj))],
        out_specs=pl.BlockSpec((TM_IN, TN_IN), lambda i, j: (i, j)),
        out_shape=jax.ShapeDtypeStruct((m, D_FF), BF16),
        scratch_shapes=[pltpu.VMEM((TM_IN, d), BF16),
                        pltpu.VMEM((TM_IN + SUBLANES, TN_IN), F32),
                        pltpu.VMEM((TM_IN + SUBLANES, TN_IN), F32),
                        pltpu.VMEM((2, nj, SUBLANES, TN_IN), F32)],
        compiler_params=pltpu.CompilerParams(
            dimension_semantics=("arbitrary", "arbitrary"),
            vmem_limit_bytes=VMEM_LIMIT),
        name="ffn_up_conv_gate",
    )(x, gain, w_up, w_up, conv_w, conv_w, conv_b, conv_b)


def _retention_decay_tables():
    lg = jnp.log(1.0 - jnp.power(2.0, -5.0 - jnp.arange(RET_HEADS, dtype=F32)))
    idx = jnp.arange(RET_CHUNK, dtype=F32)
    q_dec = jnp.exp(lg[:, None] * idx)
    k_dec = jnp.exp(-lg[:, None] * idx) * RET_HEAD_DIM ** -0.5
    dec = jnp.concatenate([q_dec, k_dec], axis=0)[..., None]
    dec = jnp.broadcast_to(dec, (2 * RET_HEADS, RET_CHUNK, LANES)).astype(F32)
    chunk_dec = jnp.exp(lg * RET_CHUNK).astype(F32)
    return dec, chunk_dec


def kernel(x, positions, w_in, w_out, w_up, w_down, conv_w, conv_b, attn_sinks,
           pre_mix_norm, post_mix_norm, attn_out_norm, ret_out_norm,
           pre_ffn_norm, post_ffn_norm):
    batch, seq, d = x.shape
    m = batch * seq
    depth = w_in.shape[0]
    xf = x.reshape(m, d)
    pos_b = jnp.broadcast_to(positions.reshape(m, 1).astype(F32), (m, LANES))
    tables = _rope_tables(pos_b)
    dec, cdec = _retention_decay_tables()
    pre_mix = pre_mix_norm[:, None]
    post_mix = post_mix_norm[:, None]
    pre_ffn = pre_ffn_norm[:, None]
    post_ffn = post_ffn_norm[:, None]
    anorm = attn_out_norm[:, None]
    rnorm = ret_out_norm.reshape(depth, 1, D_RET)
    conv_b3 = conv_b[:, None]
    for l in range(depth):
        proj = _in_proj(xf, pre_mix, w_in, l, tables, dec)
        mix = _mixer(proj, attn_sinks[l], cdec, anorm, rnorm, l, batch)
        xf = _matmul_norm_res(mix, w_out, xf, post_mix, l)
        f = _ffn_up(xf, pre_ffn, w_up, conv_w, conv_b3, l, seq)
        xf = _matmul_norm_res(f, w_down, xf, post_ffn, l)
    return xf.reshape(batch, seq, d)
```

```python
import functools

import jax
import jax.numpy as jnp
from jax import lax
from jax.experimental import pallas as pl
from jax.experimental.pallas import tpu as pltpu

F32 = jnp.float32
BF16 = jnp.bfloat16

D_MODEL = 2048
D_ATTN = 1024
HEAD_DIM = 64
N_Q_HEADS = 16
N_KV_HEADS = 4
Q_PER_KV = N_Q_HEADS // N_KV_HEADS
BLOCK = 128
ROPE_DIM = 16
ROPE_THETA = 500000.0
D_RET = 1024
RET_HEADS = 4
RET_HEAD_DIM = 256
RET_THETA = 10000.0
D_KV = N_KV_HEADS * HEAD_DIM
D_IN = D_ATTN + 2 * D_KV + 4 * D_RET
D_FF = 5632
CONV_W = 3
EPS = 1e-6
NEG_INF = -1e30
LOG2E = 1.4426950408889634

LANES = 128
SUBLANES = 8
BF16_ROWS = 16
W_CHUNK_BYTES = 4 * 1024 * 1024
VMEM_LIMIT = 56 * 1024 * 1024
VMEM_BUDGET = 44 * 1024 * 1024

TM_IN = 1024
TN_IN = 512
TM_OUT = 256
NORM_ROWS = 128
ROW_GROUP = SUBLANES * SUBLANES
MIX_ROWS = 1024
RET_CHUNK = 256
assert MIX_ROWS % RET_CHUNK == 0 and TM_OUT % RET_CHUNK == 0
GELU_C1 = 0.7978845608028654
GELU_C2 = GELU_C1 * 0.044715

Q_TILES = D_ATTN // TN_IN
KV_TILE = Q_TILES
RET_TILE0 = (D_ATTN + 2 * D_KV) // TN_IN
ROPE_R_TILES = (RET_TILE0, RET_TILE0 + 2 * D_RET // TN_IN)
HEADS_PER_TILE = TN_IN // RET_HEAD_DIM
assert 2 * D_KV == TN_IN and (D_ATTN + 2 * D_KV) % TN_IN == 0


def _rms_rows(xf, gain):
    ms = jnp.mean(xf * xf, axis=-1, keepdims=True)
    return xf * lax.rsqrt(ms + EPS) * gain


def _norm_into(x_ref, g_ref, h_ref):
    gain = g_ref[...]

    def body(r, carry):
        rows = pl.ds(pl.multiple_of(r * NORM_ROWS, NORM_ROWS), NORM_ROWS)
        h_ref[rows, :] = _rms_rows(x_ref[rows, :], gain).astype(h_ref.dtype)
        return carry

    lax.fori_loop(0, x_ref.shape[0] // NORM_ROWS, body, 0)


def _rope_table_kernel(pos_ref, inv_r_ref, inv_a_ref, rot_ref, sa_ref, sb_ref,
                       rc_ref, rs_ref, ac_ref, asa_ref, asb_ref):
    pos = pos_ref[...]
    ang_r = pos * inv_r_ref[...]
    rc_ref[...] = jnp.cos(ang_r)
    rs_ref[...] = jnp.sin(ang_r)
    ang_a = pos * inv_a_ref[...]
    ca = jnp.cos(ang_a)
    sa = jnp.sin(ang_a)
    ac_ref[...] = jnp.where(rot_ref[...] > 0.5, ca, 1.0)
    asa_ref[...] = sa * sa_ref[...]
    asb_ref[...] = sa * sb_ref[...]


def _rope_tables(pos_b):
    m = pos_b.shape[0]
    half_r = RET_HEAD_DIM // 2
    inv_r = jnp.power(F32(RET_THETA), -jnp.arange(half_r, dtype=F32) / half_r)
    half_a = ROPE_DIM // 2
    inv_a8 = jnp.power(F32(ROPE_THETA), -jnp.arange(half_a, dtype=F32) / half_a)
    lane = jnp.arange(LANES) % HEAD_DIM
    inv_a = jnp.where(lane < ROPE_DIM, inv_a8[lane % half_a], 0.0).astype(F32)
    rot = (lane < ROPE_DIM).astype(F32)
    sa = jnp.where(lane < half_a, -1.0, 0.0).astype(F32)
    sb = jnp.where((lane >= half_a) & (lane < ROPE_DIM), 1.0, 0.0).astype(F32)
    rows = 2048
    row_spec = pl.BlockSpec((rows, LANES), lambda i: (i, 0))
    lane_spec = pl.BlockSpec((1, LANES), lambda i: (0, 0))
    return pl.pallas_call(
        _rope_table_kernel,
        grid=(m // rows,),
        in_specs=[row_spec] + [lane_spec] * 5,
        out_specs=[row_spec] * 5,
        out_shape=[jax.ShapeDtypeStruct((m, LANES), F32)] * 5,
        compiler_params=pltpu.CompilerParams(
            dimension_semantics=("parallel",), vmem_limit_bytes=VMEM_LIMIT),
        name="rope_tables",
    )(pos_b, inv_r[None], inv_a[None], rot[None], sa[None], sb[None])


def _in_proj_kernel(n_load, x_ref, g_ref, w_ref, ac_ref, asa_ref, asb_ref, rc_ref, rs_ref,
                    dec_ref, o_ref, wb_ref):
    s = pl.program_id(0)

    @pl.when(s < n_load)
    def _():
        _load_weight_chunk(s, w_ref, wb_ref)

    @pl.when(s >= n_load)
    def _():
        _in_proj_tile(x_ref, g_ref, wb_ref, ac_ref, asa_ref, asb_ref, rc_ref, rs_ref, dec_ref, o_ref)


def _in_proj_tile(x_ref, g_ref, w_ref, ac_ref, asa_ref, asb_ref, rc_ref, rs_ref, dec_ref, o_ref):
    tm = x_ref.shape[0]
    h = _rms_rows(x_ref[...], g_ref[...]).astype(BF16)

    def attn_rope(acc, base, n_cols, scale):
        cos, sin_a, sin_b = ac_ref[...], asa_ref[...], asb_ref[...]
        for c in range(n_cols):
            xc = acc[:, c * LANES:(c + 1) * LANES]
            y = (xc * cos + pltpu.roll(xc, LANES - ROPE_DIM // 2, 1) * sin_a
                 + pltpu.roll(xc, ROPE_DIM // 2, 1) * sin_b)
            if scale is not None:
                y = y * scale
            o_ref[:, base + c * LANES:base + (c + 1) * LANES] = y.astype(o_ref.dtype)

    for t in range(w_ref.shape[1] // TN_IN):
        base = t * TN_IN
        acc = jnp.dot(h, w_ref[:, base:base + TN_IN], preferred_element_type=F32)
        if t < Q_TILES:
            attn_rope(acc, base, TN_IN // LANES, HEAD_DIM ** -0.5 * LOG2E)
        elif t == KV_TILE:
            attn_rope(acc, base, D_KV // LANES, None)
            o_ref[:, base + D_KV:base + TN_IN] = acc[:, D_KV:].astype(o_ref.dtype)
        elif t < ROPE_R_TILES[1]:
            half = RET_HEAD_DIM // 2
            cos, sin = rc_ref[...], rs_ref[...]
            for hh in range(HEADS_PER_TILE):
                head = (t - ROPE_R_TILES[0]) * HEADS_PER_TILE + hh
                dec = jnp.concatenate([dec_ref[head]] * (tm // RET_CHUNK), axis=0)
                lo = hh * RET_HEAD_DIM
                x1 = acc[:, lo:lo + half]
                x2 = acc[:, lo + half:lo + 2 * half]
                o_ref[:, base + lo:base + lo + half] = (
                    (x1 * cos - x2 * sin) * dec).astype(o_ref.dtype)
                o_ref[:, base + lo + half:base + lo + 2 * half] = (
                    (x2 * cos + x1 * sin) * dec).astype(o_ref.dtype)
        else:
            o_ref[:, base:base + TN_IN] = acc.astype(o_ref.dtype)


def _in_proj(x, gain, w, layer, tables, dec):
    m, d = x.shape
    n = w.shape[2]
    rc, rs, ac, asa, asb = tables
    chunk = _weight_chunk_rows(d, n)
    n_load = d // chunk

    def tile(s):
        return jnp.maximum(s - n_load, 0), 0

    tab_spec = pl.BlockSpec((TM_OUT, LANES), tile)
    return pl.pallas_call(
        functools.partial(_in_proj_kernel, n_load),
        grid=(n_load + m // TM_OUT,),
        in_specs=[pl.BlockSpec((TM_OUT, d), tile),
                  pl.BlockSpec((None, 1, d), lambda s: (layer, 0, 0)),
                  pl.BlockSpec((None, chunk, n),
                               lambda s: (layer, jnp.minimum(s, n_load - 1), 0)),
                  tab_spec, tab_spec, tab_spec, tab_spec, tab_spec,
                  pl.BlockSpec(dec.shape, lambda s: (0, 0, 0))],
        out_specs=pl.BlockSpec((TM_OUT, n), tile),
        out_shape=jax.ShapeDtypeStruct((m, n), BF16),
        scratch_shapes=[pltpu.VMEM((d, n), BF16)],
        compiler_params=pltpu.CompilerParams(
            dimension_semantics=("arbitrary",), vmem_limit_bytes=VMEM_LIMIT),
        name="norm_in_proj_rope",
    )(x, gain, w, ac, asa, asb, rc, rs, dec)


def _mixer_kernel(qa_ref, kc_ref, kp_ref, vc_ref, vp_ref,
                  qr0_ref, qr1_ref, kr0_ref, kr1_ref, vr0_ref, vr1_ref, gr0_ref, gr1_ref,
                  sink_ref, cdec_ref, anorm_ref, rnorm_ref,
                  o_ref, state_ref, ya_ref):
    n = pl.program_id(1)
    qr_refs, kr_refs = (qr0_ref, qr1_ref), (kr0_ref, kr1_ref)
    vr_refs, gr_refs = (vr0_ref, vr1_ref), (gr0_ref, gr1_ref)

    @pl.when(n == 0)
    def _():
        state_ref[...] = jnp.zeros_like(state_ref)

    lane_b = lax.broadcasted_iota(jnp.int32, (BLOCK, LANES), 1)
    low_b = lane_b < HEAD_DIM
    ones_lo = jnp.where(low_b, 1.0, 0.0).astype(BF16)
    ones_hi = jnp.where(low_b, 0.0, 1.0).astype(BF16)

    def placed(x_b):
        x = x_b.astype(F32)
        xr = pltpu.roll(x, HEAD_DIM, 1)
        views = ((jnp.where(low_b, x, 0.0), jnp.where(low_b, 0.0, xr)),
                 (jnp.where(low_b, xr, 0.0), jnp.where(low_b, 0.0, x)))
        return [tuple(t.astype(BF16) for t in pair) for pair in views]

    def kv_matrices(k_ref, v_ref, rows):
        ks, vs = [], []
        for c in range(D_KV // LANES):
            cols = slice(c * LANES, (c + 1) * LANES)
            for k2, (lo, hi) in zip(placed(k_ref[rows, cols]), placed(v_ref[rows, cols])):
                ks.append(jnp.concatenate(k2, axis=0))
                vs.append(jnp.concatenate([jnp.concatenate([lo, ones_lo], axis=1),
                                           jnp.concatenate([hi, ones_hi], axis=1)], axis=0))
        return ks, vs

    n_sub = qa_ref.shape[0] // BLOCK
    blocks = [kv_matrices(kp_ref, vp_ref, slice(0, BLOCK))]
    blocks += [kv_matrices(kc_ref, vc_ref, slice(sb * BLOCK, (sb + 1) * BLOCK))
               for sb in range(n_sub)]

    qi = lax.broadcasted_iota(jnp.int32, (BLOCK, 2 * BLOCK), 0)
    kj = lax.broadcasted_iota(jnp.int32, (BLOCK, 2 * BLOCK), 1) & (BLOCK - 1)
    in_cur = kj <= qi
    no_prev = jnp.where(n > 0, 0.0, NEG_INF)
    qk_dims = (((1,), (1,)), ((), ()))

    for sb in range(n_sub):
        rows = slice(sb * BLOCK, (sb + 1) * BLOCK)
        (k_prev, v_prev), (k_cur, v_cur) = blocks[sb], blocks[sb + 1]
        for c in range(D_ATTN // LANES):
            g = (2 * c) // Q_PER_KV
            qc = qa_ref[rows, c * LANES:(c + 1) * LANES]
            s_cur = lax.dot_general(qc, k_cur[g], qk_dims, preferred_element_type=F32)
            s_prev = lax.dot_general(qc, k_prev[g], qk_dims, preferred_element_type=F32)
            if sb == 0:
                s_prev = s_prev + no_prev
            s = jnp.where(in_cur, s_cur, s_prev)
            ps, sink_terms = [], []
            for hh in range(2):
                sh = s[:, hh * BLOCK:(hh + 1) * BLOCK]
                sink = sink_ref[2 * c + hh] * LOG2E
                mx = jnp.maximum(jnp.max(sh, axis=-1, keepdims=True), sink)
                ps.append(jnp.exp2(sh - mx))
                sink_terms.append(jnp.exp2(sink - mx))
            p = jnp.concatenate(ps, axis=1)
            acc = (jnp.dot(jnp.where(in_cur, p, 0.0).astype(BF16), v_cur[g],
                           preferred_element_type=F32)
                   + jnp.dot(jnp.where(in_cur, 0.0, p).astype(BF16), v_prev[g],
                             preferred_element_type=F32))
            denom = acc[:, LANES:] + jnp.where(low_b, sink_terms[0], sink_terms[1])
            ya_ref[rows, c * LANES:(c + 1) * LANES] = acc[:, :LANES] / denom

    o_ref[:, :D_ATTN] = _rms_rows(ya_ref[...], anorm_ref[...]).astype(o_ref.dtype)

    ri = lax.broadcasted_iota(jnp.int32, (RET_CHUNK, RET_CHUNK), 0)
    rj = lax.broadcasted_iota(jnp.int32, (RET_CHUNK, RET_CHUNK), 1)
    causal = ri >= rj
    for h in range(RET_HEADS):
        cols = slice(h * RET_HEAD_DIM, (h + 1) * RET_HEAD_DIM)
        pair, side = divmod(h, HEADS_PER_TILE)
        pcols = slice(side * RET_HEAD_DIM, (side + 1) * RET_HEAD_DIM)
        state = state_ref[h]
        for r0 in range(0, qa_ref.shape[0], RET_CHUNK):
            rows = slice(r0, r0 + RET_CHUNK)
            q = qr_refs[pair][rows, pcols]
            k = kr_refs[pair][rows, pcols]
            v = vr_refs[pair][rows, pcols]
            sc = lax.dot_general(q, k, qk_dims, preferred_element_type=F32)
            sc = jnp.where(causal, sc, 0.0).astype(BF16)
            y = (jnp.dot(sc, v, preferred_element_type=F32)
                 + jnp.dot(q, state.astype(BF16), preferred_element_type=F32))
            kv = lax.dot_general(k, v, (((0,), (0,)), ((), ())), preferred_element_type=F32)
            state = (state + kv) * cdec_ref[h]
            y = _rms_rows(y, rnorm_ref[:, cols])
            gate = gr_refs[pair][rows, pcols].astype(F32)
            o_ref[rows, D_ATTN + h * RET_HEAD_DIM:D_ATTN + (h + 1) * RET_HEAD_DIM] = (
                y * (gate * jax.nn.sigmoid(gate))).astype(o_ref.dtype)
        state_ref[h] = state


def _mixer(proj, sinks, cdec, anorm, rnorm, layer, batch):
    m = proj.shape[0]
    steps = m // batch // MIX_ROWS
    sub = MIX_ROWS // BLOCK

    def rows(width, col):
        return pl.BlockSpec((MIX_ROWS, width), lambda b, n: (b * steps + n, col))

    def prev_rows(width, col):
        return pl.BlockSpec(
            (BLOCK, width), lambda b, n: ((b * steps + n) * sub - jnp.minimum(n, 1), col))

    k_col = D_ATTN // D_KV
    ret_specs = [rows(TN_IN, RET_TILE0 + t) for t in range(4 * D_RET // TN_IN)]
    smem = pl.BlockSpec(memory_space=pltpu.SMEM)
    return pl.pallas_call(
        _mixer_kernel,
        grid=(batch, steps),
        in_specs=[rows(D_ATTN, 0),
                  rows(D_KV, k_col), prev_rows(D_KV, k_col),
                  rows(D_KV, k_col + 1), prev_rows(D_KV, k_col + 1),
                  *ret_specs,
                  smem, smem,
                  pl.BlockSpec((None, 1, D_ATTN), lambda b, n: (layer, 0, 0)),
                  pl.BlockSpec((None, 1, D_RET), lambda b, n: (layer, 0, 0))],
        out_specs=rows(D_ATTN + D_RET, 0),
        out_shape=jax.ShapeDtypeStruct((m, D_ATTN + D_RET), BF16),
        scratch_shapes=[pltpu.VMEM((RET_HEADS, RET_HEAD_DIM, RET_HEAD_DIM), F32),
                        pltpu.VMEM((MIX_ROWS, D_ATTN), F32)],
        compiler_params=pltpu.CompilerParams(
            dimension_semantics=("parallel", "arbitrary"),
            vmem_limit_bytes=VMEM_LIMIT),
        name="token_mixer",
    )(*([proj] * (5 + len(ret_specs))), sinks, cdec, anorm, rnorm)


def _weight_chunk_rows(k, n):
    rows = BF16_ROWS
    while rows * 2 * n * 4 <= W_CHUNK_BYTES and k % (rows * 2) == 0:
        rows *= 2
    return rows


def _norm_into_transposed_groups(x_ref, g_ref, h32_ref, h_ref):
    gain = g_ref[...]
    n_cols = x_ref.shape[1] // LANES
    for r0 in range(0, x_ref.shape[0], NORM_ROWS):
        y = _rms_rows(x_ref[r0:r0 + NORM_ROWS, :], gain)
        for g0 in range(0, NORM_ROWS, ROW_GROUP):
            for s in range(SUBLANES):
                rows = slice(g0 + s * SUBLANES, g0 + (s + 1) * SUBLANES)
                for c in range(n_cols):
                    h32_ref[c, pl.ds(r0 + g0 + s, SUBLANES, stride=SUBLANES), :] = (
                        y[rows, c * LANES:(c + 1) * LANES])
    for c in range(n_cols):
        h_ref[:, c * LANES:(c + 1) * LANES] = h32_ref[c].astype(h_ref.dtype)


def _load_weight_chunk(step, w_ref, wb_ref):
    ck = w_ref.shape[0]
    rows = pl.ds(pl.multiple_of(step * ck, ck), ck)
    wb_ref[rows, :] = w_ref[...].astype(wb_ref.dtype)


def _matmul_norm_res_kernel(n_load, a_ref, w_ref, x_ref, g_ref, o_ref, wb_ref):
    s = pl.program_id(0)

    @pl.when(s < n_load)
    def _():
        _load_weight_chunk(s, w_ref, wb_ref)

    @pl.when(s >= n_load)
    def _():
        for r0 in range(0, a_ref.shape[0], TM_OUT):
            rows = slice(r0, r0 + TM_OUT)
            y = jnp.dot(a_ref[rows, :], wb_ref[...], preferred_element_type=F32)
            o_ref[rows, :] = x_ref[rows, :] + _rms_rows(y, g_ref[...])


def _matmul_norm_res(a, w, x, gain, layer):
    m, k = a.shape
    d = w.shape[2]
    chunk = _weight_chunk_rows(k, d)
    n_load = k // chunk
    resident = k * d * 2 + 2 * chunk * d * 4
    per_row = 2 * (k * 2 + d * 4 + d * 4)
    tm = TM_OUT
    while resident + 2 * tm * per_row <= VMEM_BUDGET and m % (2 * tm) == 0 and tm < TM_IN:
        tm *= 2

    def tile(s):
        return jnp.maximum(s - n_load, 0), 0

    return pl.pallas_call(
        functools.partial(_matmul_norm_res_kernel, n_load),
        grid=(n_load + m // tm,),
        in_specs=[pl.BlockSpec((tm, k), tile),
                  pl.BlockSpec((None, chunk, d),
                               lambda s: (layer, jnp.minimum(s, n_load - 1), 0)),
                  pl.BlockSpec((tm, d), tile),
                  pl.BlockSpec((None, 1, d), lambda s: (layer, 0, 0))],
        out_specs=pl.BlockSpec((tm, d), tile),
        out_shape=jax.ShapeDtypeStruct((m, d), F32),
        scratch_shapes=[pltpu.VMEM((k, d), BF16)],
        compiler_params=pltpu.CompilerParams(
            dimension_semantics=("arbitrary",), vmem_limit_bytes=VMEM_LIMIT),
        name="proj_norm_residual",
    )(a, w, x, gain)


def _ffn_up_kernel(tiles_per_seq, x_ref, g_ref, wa_ref, wg_ref, cwa_ref, cwg_ref,
                   cba_ref, cbg_ref, o_ref, h_ref, h32_ref, stage_ref, carry_ref):
    i = pl.program_id(0)
    j = pl.program_id(1)
    tm, tn = o_ref.shape
    groups = tm // ROW_GROUP

    @pl.when(j == 0)
    def _():
        _norm_into_transposed_groups(x_ref, g_ref, h32_ref, h_ref)

    @pl.when((i == 0) & (j == 0))
    def _():
        carry_ref[...] = jnp.zeros_like(carry_ref)

    starts_sequence = (i % tiles_per_seq) == 0
    h = h_ref[...]
    last_sublane = lax.broadcasted_iota(jnp.int32, (groups, SUBLANES, tn), 1) == SUBLANES - 1

    def conv(w_ref, cw, cb, slot):
        u = jnp.dot(h, w_ref[...].astype(BF16), preferred_element_type=F32)
        ug = u.reshape(groups, SUBLANES, SUBLANES, tn)
        halo = jnp.where(starts_sequence, 0.0, carry_ref[slot, j])
        carry_ref[slot, j] = ug[groups - 1, SUBLANES - 2:]

        def wrapped(v):
            above = jnp.concatenate([halo[v - (SUBLANES - 2)][None], ug[:groups - 1, v]], axis=0)
            return pltpu.roll(jnp.where(last_sublane, above, ug[:, v]), 1, 1)

        d1_first = wrapped(SUBLANES - 1)[:, None]
        d2_first = wrapped(SUBLANES - 2)[:, None]
        d1 = jnp.concatenate([d1_first, ug[:, :SUBLANES - 1]], axis=1)
        d2 = jnp.concatenate([d2_first, d1_first, ug[:, :SUBLANES - 2]], axis=1)
        y = cb + d2 * cw[0:1] + d1 * cw[1:2] + ug * cw[2:3]
        return y.reshape(tm, tn)

    a = conv(wa_ref, cwa_ref[...], cba_ref[...], 0)
    half_gate = conv(wg_ref, cwg_ref[...] * 0.5, cbg_ref[...] * 0.5, 1)
    t = jnp.tanh(a * (GELU_C1 + GELU_C2 * (a * a)))
    f = (a + a * t) * half_gate
    for c in range(tn // LANES):
        for r0 in range(0, tm, SUBLANES):
            g0, v = divmod(r0, ROW_GROUP)
            stage_ref[c, pl.ds(g0 * ROW_GROUP + v // SUBLANES, SUBLANES, stride=SUBLANES), :] = (
                f[r0:r0 + SUBLANES, c * LANES:(c + 1) * LANES])
        o_ref[:, c * LANES:(c + 1) * LANES] = stage_ref[c].astype(o_ref.dtype)


def _ffn_up(x, gain, w_up, conv_w, conv_b, layer, seq):
    m, d = x.shape
    nj = D_FF // TN_IN
    kern = functools.partial(_ffn_up_kernel, seq // TM_IN)
    return pl.pallas_call(
        kern,
        grid=(m // TM_IN, nj),
        in_specs=[pl.BlockSpec((TM_IN, d), lambda i, j: (i, 0)),
                  pl.BlockSpec((None, 1, d), lambda i, j: (layer, 0, 0)),
                  pl.BlockSpec((None, d, TN_IN), lambda i, j: (layer, 0, j)),
                  pl.BlockSpec((None, d, TN_IN), lambda i, j: (layer, 0, j + nj)),
                  pl.BlockSpec((None, CONV_W, TN_IN), lambda i, j: (layer, 0, j)),
                  pl.BlockSpec((None, CONV_W, TN_IN), lambda i, j: (layer, 0, j + nj)),
                  pl.BlockSpec((None, 1, TN_IN), lambda i, j: (layer, 0, j)),
                  pl.BlockSpec((None, 1, TN_IN), lambda i, j: (layer, 0, j + nj))],
        out_specs=pl.BlockSpec((TM_IN, TN_IN), lambda i, j: (i, j)),
        out_shape=jax.ShapeDtypeStruct((m, D_FF), BF16),
        scratch_shapes=[pltpu.VMEM((TM_IN, d), BF16),
                        pltpu.VMEM((d // LANES, TM_IN, LANES), F32),
                        pltpu.VMEM((TN_IN // LANES, TM_IN, LANES), F32),
                        pltpu.VMEM((2, nj, 2, SUBLANES, TN_IN), F32)],
        compiler_params=pltpu.CompilerParams(
            dimension_semantics=("arbitrary", "arbitrary"),
            vmem_limit_bytes=VMEM_LIMIT),
        name="ffn_up_conv_gate",
    )(x, gain, w_up, w_up, conv_w, conv_w, conv_b, conv_b)


def _retention_decay_tables():
    lg = jnp.log(1.0 - jnp.power(2.0, -5.0 - jnp.arange(RET_HEADS, dtype=F32)))
    idx = jnp.arange(RET_CHUNK, dtype=F32)
    q_dec = jnp.exp(lg[:, None] * idx)
    k_dec = jnp.exp(-lg[:, None] * idx) * RET_HEAD_DIM ** -0.5
    dec = jnp.concatenate([q_dec, k_dec], axis=0)[..., None]
    dec = jnp.broadcast_to(dec, (2 * RET_HEADS, RET_CHUNK, LANES)).astype(F32)
    chunk_dec = jnp.exp(lg * RET_CHUNK).astype(F32)
    return dec, chunk_dec


def kernel(x, positions, w_in, w_out, w_up, w_down, conv_w, conv_b, attn_sinks,
           pre_mix_norm, post_mix_norm, attn_out_norm, ret_out_norm,
           pre_ffn_norm, post_ffn_norm):
    batch, seq, d = x.shape
    m = batch * seq
    depth = w_in.shape[0]
    xf = x.reshape(m, d)
    pos_b = jnp.broadcast_to(positions.reshape(m, 1).astype(F32), (m, LANES))
    tables = _rope_tables(pos_b)
    dec, cdec = _retention_decay_tables()
    pre_mix = pre_mix_norm[:, None]
    post_mix = post_mix_norm[:, None]
    pre_ffn = pre_ffn_norm[:, None]
    post_ffn = post_ffn_norm[:, None]
    anorm = attn_out_norm[:, None]
    rnorm = ret_out_norm.reshape(depth, 1, D_RET)
    conv_b3 = conv_b[:, None]
    for l in range(depth):
        proj = _in_proj(xf, pre_mix, w_in, l, tables, dec)
        mix = _mixer(proj, attn_sinks[l], cdec, anorm, rnorm, l, batch)
        xf = _matmul_norm_res(mix, w_out, xf, post_mix, l)
        f = _ffn_up(xf, pre_ffn, w_up, conv_w, conv_b3, l, seq)
        xf = _matmul_norm_res(f, w_down, xf, post_ffn, l)
    return xf.reshape(batch, seq, d)
```

```python
import functools

import jax
import jax.numpy as jnp
from jax import lax
from jax.experimental import pallas as pl
from jax.experimental.pallas import tpu as pltpu

F32 = jnp.float32
BF16 = jnp.bfloat16

D_MODEL = 2048
D_ATTN = 1024
HEAD_DIM = 64
N_Q_HEADS = 16
N_KV_HEADS = 4
Q_PER_KV = N_Q_HEADS // N_KV_HEADS
BLOCK = 128
ROPE_DIM = 16
ROPE_THETA = 500000.0
D_RET = 1024
RET_HEADS = 4
RET_HEAD_DIM = 256
RET_THETA = 10000.0
D_KV = N_KV_HEADS * HEAD_DIM
D_IN = D_ATTN + 2 * D_KV + 4 * D_RET
D_FF = 5632
CONV_W = 3
EPS = 1e-6
NEG_INF = -1e30
LOG2E = 1.4426950408889634

LANES = 128
SUBLANES = 8
BF16_ROWS = 16
W_CHUNK_BYTES = 4 * 1024 * 1024
VMEM_LIMIT = 56 * 1024 * 1024
VMEM_BUDGET = 44 * 1024 * 1024

TM_IN = 1024
TN_IN = 512
TM_OUT = 256
NORM_ROWS = 128
MIX_ROWS = 1024
RET_CHUNK = 256
assert MIX_ROWS % RET_CHUNK == 0 and TM_OUT % RET_CHUNK == 0
GELU_C1 = 0.7978845608028654
GELU_C2 = GELU_C1 * 0.044715

Q_TILES = D_ATTN // TN_IN
KV_TILE = Q_TILES
RET_TILE0 = (D_ATTN + 2 * D_KV) // TN_IN
ROPE_R_TILES = (RET_TILE0, RET_TILE0 + 2 * D_RET // TN_IN)
HEADS_PER_TILE = TN_IN // RET_HEAD_DIM
assert 2 * D_KV == TN_IN and (D_ATTN + 2 * D_KV) % TN_IN == 0


def _rms_rows(xf, gain):
    ms = jnp.mean(xf * xf, axis=-1, keepdims=True)
    return xf * lax.rsqrt(ms + EPS) * gain


def _norm_into(x_ref, g_ref, h_ref):
    gain = g_ref[...]

    def body(r, carry):
        rows = pl.ds(pl.multiple_of(r * NORM_ROWS, NORM_ROWS), NORM_ROWS)
        h_ref[rows, :] = _rms_rows(x_ref[rows, :], gain).astype(h_ref.dtype)
        return carry

    lax.fori_loop(0, x_ref.shape[0] // NORM_ROWS, body, 0)


def _rope_table_kernel(pos_ref, inv_r_ref, inv_a_ref, rot_ref, sa_ref, sb_ref,
                       rc_ref, rs_ref, ac_ref, asa_ref, asb_ref):
    pos = pos_ref[...]
    ang_r = pos * inv_r_ref[...]
    rc_ref[...] = jnp.cos(ang_r)
    rs_ref[...] = jnp.sin(ang_r)
    ang_a = pos * inv_a_ref[...]
    ca = jnp.cos(ang_a)
    sa = jnp.sin(ang_a)
    ac_ref[...] = jnp.where(rot_ref[...] > 0.5, ca, 1.0)
    asa_ref[...] = sa * sa_ref[...]
    asb_ref[...] = sa * sb_ref[...]


def _rope_tables(pos_b):
    m = pos_b.shape[0]
    half_r = RET_HEAD_DIM // 2
    inv_r = jnp.power(F32(RET_THETA), -jnp.arange(half_r, dtype=F32) / half_r)
    half_a = ROPE_DIM // 2
    inv_a8 = jnp.power(F32(ROPE_THETA), -jnp.arange(half_a, dtype=F32) / half_a)
    lane = jnp.arange(LANES) % HEAD_DIM
    inv_a = jnp.where(lane < ROPE_DIM, inv_a8[lane % half_a], 0.0).astype(F32)
    rot = (lane < ROPE_DIM).astype(F32)
    sa = jnp.where(lane < half_a, -1.0, 0.0).astype(F32)
    sb = jnp.where((lane >= half_a) & (lane < ROPE_DIM), 1.0, 0.0).astype(F32)
    rows = 2048
    row_spec = pl.BlockSpec((rows, LANES), lambda i: (i, 0))
    lane_spec = pl.BlockSpec((1, LANES), lambda i: (0, 0))
    return pl.pallas_call(
        _rope_table_kernel,
        grid=(m // rows,),
        in_specs=[row_spec] + [lane_spec] * 5,
        out_specs=[row_spec] * 5,
        out_shape=[jax.ShapeDtypeStruct((m, LANES), F32)] * 5,
        compiler_params=pltpu.CompilerParams(
            dimension_semantics=("parallel",), vmem_limit_bytes=VMEM_LIMIT),
        name="rope_tables",
    )(pos_b, inv_r[None], inv_a[None], rot[None], sa[None], sb[None])


def _in_proj_kernel(n_load, x_ref, g_ref, w_ref, ac_ref, asa_ref, asb_ref, rc_ref, rs_ref,
                    dec_ref, o_ref, wb_ref):
    s = pl.program_id(0)

    @pl.when(s < n_load)
    def _():
        _load_weight_chunk(s, w_ref, wb_ref)

    @pl.when(s >= n_load)
    def _():
        _in_proj_tile(x_ref, g_ref, wb_ref, ac_ref, asa_ref, asb_ref, rc_ref, rs_ref, dec_ref, o_ref)


def _in_proj_tile(x_ref, g_ref, w_ref, ac_ref, asa_ref, asb_ref, rc_ref, rs_ref, dec_ref, o_ref):
    tm = x_ref.shape[0]
    h = _rms_rows(x_ref[...], g_ref[...]).astype(BF16)

    def attn_rope(acc, base, n_cols, scale):
        cos, sin_a, sin_b = ac_ref[...], asa_ref[...], asb_ref[...]
        for c in range(n_cols):
            xc = acc[:, c * LANES:(c + 1) * LANES]
            y = (xc * cos + pltpu.roll(xc, LANES - ROPE_DIM // 2, 1) * sin_a
                 + pltpu.roll(xc, ROPE_DIM // 2, 1) * sin_b)
            if scale is not None:
                y = y * scale
            o_ref[:, base + c * LANES:base + (c + 1) * LANES] = y.astype(o_ref.dtype)

    for t in range(w_ref.shape[1] // TN_IN):
        base = t * TN_IN
        acc = jnp.dot(h, w_ref[:, base:base + TN_IN], preferred_element_type=F32)
        if t < Q_TILES:
            attn_rope(acc, base, TN_IN // LANES, HEAD_DIM ** -0.5 * LOG2E)
        elif t == KV_TILE:
            attn_rope(acc, base, D_KV // LANES, None)
            o_ref[:, base + D_KV:base + TN_IN] = acc[:, D_KV:].astype(o_ref.dtype)
        elif t < ROPE_R_TILES[1]:
            half = RET_HEAD_DIM // 2
            cos, sin = rc_ref[...], rs_ref[...]
            for hh in range(HEADS_PER_TILE):
                head = (t - ROPE_R_TILES[0]) * HEADS_PER_TILE + hh
                dec = jnp.concatenate([dec_ref[head]] * (tm // RET_CHUNK), axis=0)
                lo = hh * RET_HEAD_DIM
                x1 = acc[:, lo:lo + half]
                x2 = acc[:, lo + half:lo + 2 * half]
                o_ref[:, base + lo:base + lo + half] = (
                    (x1 * cos - x2 * sin) * dec).astype(o_ref.dtype)
                o_ref[:, base + lo + half:base + lo + 2 * half] = (
                    (x2 * cos + x1 * sin) * dec).astype(o_ref.dtype)
        else:
            o_ref[:, base:base + TN_IN] = acc.astype(o_ref.dtype)


def _in_proj(x, gain, w, layer, tables, dec):
    m, d = x.shape
    n = w.shape[2]
    rc, rs, ac, asa, asb = tables
    chunk = _weight_chunk_rows(d, n)
    n_load = d // chunk

    def tile(s):
        return jnp.maximum(s - n_load, 0), 0

    tab_spec = pl.BlockSpec((TM_OUT, LANES), tile)
    return pl.pallas_call(
        functools.partial(_in_proj_kernel, n_load),
        grid=(n_load + m // TM_OUT,),
        in_specs=[pl.BlockSpec((TM_OUT, d), tile),
                  pl.BlockSpec((None, 1, d), lambda s: (layer, 0, 0)),
                  pl.BlockSpec((None, chunk, n),
                               lambda s: (layer, jnp.minimum(s, n_load - 1), 0)),
                  tab_spec, tab_spec, tab_spec, tab_spec, tab_spec,
                  pl.BlockSpec(dec.shape, lambda s: (0, 0, 0))],
        out_specs=pl.BlockSpec((TM_OUT, n), tile),
        out_shape=jax.ShapeDtypeStruct((m, n), BF16),
        scratch_shapes=[pltpu.VMEM((d, n), BF16)],
        compiler_params=pltpu.CompilerParams(
            dimension_semantics=("arbitrary",), vmem_limit_bytes=VMEM_LIMIT),
        name="norm_in_proj_rope",
    )(x, gain, w, ac, asa, asb, rc, rs, dec)


def _mixer_kernel(qa_ref, kc_ref, kp_ref, vc_ref, vp_ref,
                  qr0_ref, qr1_ref, kr0_ref, kr1_ref, vr0_ref, vr1_ref, gr0_ref, gr1_ref,
                  sink_ref, cdec_ref, anorm_ref, rnorm_ref,
                  o_ref, state_ref, ya_ref):
    n = pl.program_id(1)
    qr_refs, kr_refs = (qr0_ref, qr1_ref), (kr0_ref, kr1_ref)
    vr_refs, gr_refs = (vr0_ref, vr1_ref), (gr0_ref, gr1_ref)

    @pl.when(n == 0)
    def _():
        state_ref[...] = jnp.zeros_like(state_ref)

    lane_b = lax.broadcasted_iota(jnp.int32, (BLOCK, LANES), 1)
    low_b = lane_b < HEAD_DIM
    ones_lo = jnp.where(low_b, 1.0, 0.0).astype(BF16)
    ones_hi = jnp.where(low_b, 0.0, 1.0).astype(BF16)

    def placed(x_b):
        x = x_b.astype(F32)
        xr = pltpu.roll(x, HEAD_DIM, 1)
        views = ((jnp.where(low_b, x, 0.0), jnp.where(low_b, 0.0, xr)),
                 (jnp.where(low_b, xr, 0.0), jnp.where(low_b, 0.0, x)))
        return [tuple(t.astype(BF16) for t in pair) for pair in views]

    def kv_matrices(k_ref, v_ref, rows):
        ks, vs = [], []
        for c in range(D_KV // LANES):
            cols = slice(c * LANES, (c + 1) * LANES)
            for k2, (lo, hi) in zip(placed(k_ref[rows, cols]), placed(v_ref[rows, cols])):
                ks.append(jnp.concatenate(k2, axis=0))
                vs.append(jnp.concatenate([jnp.concatenate([lo, ones_lo], axis=1),
                                           jnp.concatenate([hi, ones_hi], axis=1)], axis=0))
        return ks, vs

    n_sub = qa_ref.shape[0] // BLOCK
    blocks = [kv_matrices(kp_ref, vp_ref, slice(0, BLOCK))]
    blocks += [kv_matrices(kc_ref, vc_ref, slice(sb * BLOCK, (sb + 1) * BLOCK))
               for sb in range(n_sub)]

    qi = lax.broadcasted_iota(jnp.int32, (BLOCK, 2 * BLOCK), 0)
    kj = lax.broadcasted_iota(jnp.int32, (BLOCK, 2 * BLOCK), 1) & (BLOCK - 1)
    in_cur = kj <= qi
    no_prev = jnp.where(n > 0, 0.0, NEG_INF)
    qk_dims = (((1,), (1,)), ((), ()))

    for sb in range(n_sub):
        rows = slice(sb * BLOCK, (sb + 1) * BLOCK)
        (k_prev, v_prev), (k_cur, v_cur) = blocks[sb], blocks[sb + 1]
        for c in range(D_ATTN // LANES):
            g = (2 * c) // Q_PER_KV
            qc = qa_ref[rows, c * LANES:(c + 1) * LANES]
            s_cur = lax.dot_general(qc, k_cur[g], qk_dims, preferred_element_type=F32)
            s_prev = lax.dot_general(qc, k_prev[g], qk_dims, preferred_element_type=F32)
            if sb == 0:
                s_prev = s_prev + no_prev
            s = jnp.where(in_cur, s_cur, s_prev)
            ps, sink_terms = [], []
            for hh in range(2):
                sh = s[:, hh * BLOCK:(hh + 1) * BLOCK]
                sink = sink_ref[2 * c + hh] * LOG2E
                mx = jnp.maximum(jnp.max(sh, axis=-1, keepdims=True), sink)
                ps.append(jnp.exp2(sh - mx))
                sink_terms.append(jnp.exp2(sink - mx))
            p = jnp.concatenate(ps, axis=1)
            acc = (jnp.dot(jnp.where(in_cur, p, 0.0).astype(BF16), v_cur[g],
                           preferred_element_type=F32)
                   + jnp.dot(jnp.where(in_cur, 0.0, p).astype(BF16), v_prev[g],
                             preferred_element_type=F32))
            denom = acc[:, LANES:] + jnp.where(low_b, sink_terms[0], sink_terms[1])
            ya_ref[rows, c * LANES:(c + 1) * LANES] = acc[:, :LANES] / denom

    o_ref[:, :D_ATTN] = _rms_rows(ya_ref[...], anorm_ref[...]).astype(o_ref.dtype)

    ri = lax.broadcasted_iota(jnp.int32, (RET_CHUNK, RET_CHUNK), 0)
    rj = lax.broadcasted_iota(jnp.int32, (RET_CHUNK, RET_CHUNK), 1)
    causal = ri >= rj
    for h in range(RET_HEADS):
        cols = slice(h * RET_HEAD_DIM, (h + 1) * RET_HEAD_DIM)
        pair, side = divmod(h, HEADS_PER_TILE)
        pcols = slice(side * RET_HEAD_DIM, (side + 1) * RET_HEAD_DIM)
        state = state_ref[h]
        for r0 in range(0, qa_ref.shape[0], RET_CHUNK):
            rows = slice(r0, r0 + RET_CHUNK)
            q = qr_refs[pair][rows, pcols]
            k = kr_refs[pair][rows, pcols]
            v = vr_refs[pair][rows, pcols]
            sc = lax.dot_general(q, k, qk_dims, preferred_element_type=F32)
            sc = jnp.where(causal, sc, 0.0).astype(BF16)
            y = (jnp.dot(sc, v, preferred_element_type=F32)
                 + jnp.dot(q, state.astype(BF16), preferred_element_type=F32))
            kv = lax.dot_general(k, v, (((0,), (0,)), ((), ())), preferred_element_type=F32)
            state = (state + kv) * cdec_ref[h]
            y = _rms_rows(y, rnorm_ref[:, cols])
            gate = gr_refs[pair][rows, pcols].astype(F32)
            o_ref[rows, D_ATTN + h * RET_HEAD_DIM:D_ATTN + (h + 1) * RET_HEAD_DIM] = (
                y * (gate * jax.nn.sigmoid(gate))).astype(o_ref.dtype)
        state_ref[h] = state


def _mixer(proj, sinks, cdec, anorm, rnorm, layer, batch):
    m = proj.shape[0]
    steps = m // batch // MIX_ROWS
    sub = MIX_ROWS // BLOCK

    def rows(width, col):
        return pl.BlockSpec((MIX_ROWS, width), lambda b, n: (b * steps + n, col))

    def prev_rows(width, col):
        return pl.BlockSpec(
            (BLOCK, width), lambda b, n: ((b * steps + n) * sub - jnp.minimum(n, 1), col))

    k_col = D_ATTN // D_KV
    ret_specs = [rows(TN_IN, RET_TILE0 + t) for t in range(4 * D_RET // TN_IN)]
    smem = pl.BlockSpec(memory_space=pltpu.SMEM)
    return pl.pallas_call(
        _mixer_kernel,
        grid=(batch, steps),
        in_specs=[rows(D_ATTN, 0),
                  rows(D_KV, k_col), prev_rows(D_KV, k_col),
                  rows(D_KV, k_col + 1), prev_rows(D_KV, k_col + 1),
                  *ret_specs,
                  smem, smem,
                  pl.BlockSpec((None, 1, D_ATTN), lambda b, n: (layer, 0, 0)),
                  pl.BlockSpec((None, 1, D_RET), lambda b, n: (layer, 0, 0))],
        out_specs=rows(D_ATTN + D_RET, 0),
        out_shape=jax.ShapeDtypeStruct((m, D_ATTN + D_RET), BF16),
        scratch_shapes=[pltpu.VMEM((RET_HEADS, RET_HEAD_DIM, RET_HEAD_DIM), F32),
                        pltpu.VMEM((MIX_ROWS, D_ATTN), F32)],
        compiler_params=pltpu.CompilerParams(
            dimension_semantics=("parallel", "arbitrary"),
            vmem_limit_bytes=VMEM_LIMIT),
        name="token_mixer",
    )(*([proj] * (5 + len(ret_specs))), sinks, cdec, anorm, rnorm)


def _weight_chunk_rows(k, n):
    rows = BF16_ROWS
    while rows * 2 * n * 4 <= W_CHUNK_BYTES and k % (rows * 2) == 0:
        rows *= 2
    return rows


def _load_weight_chunk(step, w_ref, wb_ref):
    ck = w_ref.shape[0]
    rows = pl.ds(pl.multiple_of(step * ck, ck), ck)
    wb_ref[rows, :] = w_ref[...].astype(wb_ref.dtype)


def _matmul_norm_res_kernel(n_load, a_ref, w_ref, x_ref, g_ref, o_ref, wb_ref):
    s = pl.program_id(0)

    @pl.when(s < n_load)
    def _():
        _load_weight_chunk(s, w_ref, wb_ref)

    @pl.when(s >= n_load)
    def _():
        for r0 in range(0, a_ref.shape[0], TM_OUT):
            rows = slice(r0, r0 + TM_OUT)
            y = jnp.dot(a_ref[rows, :], wb_ref[...], preferred_element_type=F32)
            o_ref[rows, :] = x_ref[rows, :] + _rms_rows(y, g_ref[...])


def _matmul_norm_res(a, w, x, gain, layer):
    m, k = a.shape
    d = w.shape[2]
    chunk = _weight_chunk_rows(k, d)
    n_load = k // chunk
    resident = k * d * 2 + 2 * chunk * d * 4
    per_row = 2 * (k * 2 + d * 4 + d * 4)
    tm = TM_OUT
    while resident + 2 * tm * per_row <= VMEM_BUDGET and m % (2 * tm) == 0 and tm < TM_IN:
        tm *= 2

    def tile(s):
        return jnp.maximum(s - n_load, 0), 0

    return pl.pallas_call(
        functools.partial(_matmul_norm_res_kernel, n_load),
        grid=(n_load + m // tm,),
        in_specs=[pl.BlockSpec((tm, k), tile),
                  pl.BlockSpec((None, chunk, d),
                               lambda s: (layer, jnp.minimum(s, n_load - 1), 0)),
                  pl.BlockSpec((tm, d), tile),
                  pl.BlockSpec((None, 1, d), lambda s: (layer, 0, 0))],
        out_specs=pl.BlockSpec((tm, d), tile),
        out_shape=jax.ShapeDtypeStruct((m, d), F32),
        scratch_shapes=[pltpu.VMEM((k, d), BF16)],
        compiler_params=pltpu.CompilerParams(
            dimension_semantics=("arbitrary",), vmem_limit_bytes=VMEM_LIMIT),
        name="proj_norm_residual",
    )(a, w, x, gain)


def _ffn_up_kernel(tiles_per_seq, x_ref, g_ref, wa_ref, wg_ref, cwa_ref, cwg_ref,
                   cba_ref, cbg_ref, o_ref, h_ref, ua_ref, ug_ref, carry_ref):
    i = pl.program_id(0)
    j = pl.program_id(1)
    tm = x_ref.shape[0]

    @pl.when(j == 0)
    def _():
        _norm_into(x_ref, g_ref, h_ref)

    @pl.when((i == 0) & (j == 0))
    def _():
        carry_ref[...] = jnp.zeros_like(carry_ref)

    starts_sequence = (i % tiles_per_seq) == 0
    h = h_ref[...]

    def conv(u_ref, w_ref, cw, cb, slot):
        u = jnp.dot(h, w_ref[...].astype(BF16), preferred_element_type=F32)
        u_ref[0:SUBLANES, :] = jnp.where(starts_sequence, 0.0, carry_ref[slot, j])
        u_ref[SUBLANES:SUBLANES + tm, :] = u
        carry_ref[slot, j] = u[tm - SUBLANES:, :]
        return (cb + u_ref[SUBLANES - 2:SUBLANES - 2 + tm, :] * cw[0:1]
                + u_ref[SUBLANES - 1:SUBLANES - 1 + tm, :] * cw[1:2] + u * cw[2:3])

    a = conv(ua_ref, wa_ref, cwa_ref[...], cba_ref[...], 0)
    half_gate = conv(ug_ref, wg_ref, cwg_ref[...] * 0.5, cbg_ref[...] * 0.5, 1)
    t = jnp.tanh(a * (GELU_C1 + GELU_C2 * (a * a)))
    o_ref[...] = ((a + a * t) * half_gate).astype(o_ref.dtype)


def _ffn_up(x, gain, w_up, conv_w, conv_b, layer, seq):
    m, d = x.shape
    nj = D_FF // TN_IN
    kern = functools.partial(_ffn_up_kernel, seq // TM_IN)
    return pl.pallas_call(
        kern,
        grid=(m // TM_IN, nj),
        in_specs=[pl.BlockSpec((TM_IN, d), lambda i, j: (i, 0)),
                  pl.BlockSpec((None, 1, d), lambda i, j: (layer, 0, 0)),
                  pl.BlockSpec((None, d, TN_IN), lambda i, j: (layer, 0, j)),
                  pl.BlockSpec((None, d, TN_IN), lambda i, j: (layer, 0, j + nj)),
                  pl.BlockSpec((None, CONV_W, TN_IN), lambda i, j: (layer, 0, j)),
                  pl.BlockSpec((None, CONV_W, TN_IN), lambda i, j: (layer, 0, j + nj)),
                  pl.BlockSpec((None, 1, TN_IN), lambda i, j: (layer, 0, j)),
                  pl.BlockSpec((None, 1, TN_IN), lambda i, j: (layer, 0, j + nj))],
        out_specs=pl.BlockSpec((TM_IN, TN_IN), lambda i, j: (i, j)),
        out_shape=jax.ShapeDtypeStruct((m, D_FF), BF16),
        scratch_shapes=[pltpu.VMEM((TM_IN, d), BF16),
                        pltpu.VMEM((TM_IN + SUBLANES, TN_IN), F32),
                        pltpu.VMEM((TM_IN + SUBLANES, TN_IN), F32),
                        pltpu.VMEM((2, nj, SUBLANES, TN_IN), F32)],
        compiler_params=pltpu.CompilerParams(
            dimension_semantics=("arbitrary", "arbitrary"),
            vmem_limit_bytes=VMEM_LIMIT),
        name="ffn_up_conv_gate",
    )(x, gain, w_up, w_up, conv_w, conv_w, conv_b, conv_b)


def _retention_decay_tables():
    lg = jnp.log(1.0 - jnp.power(2.0, -5.0 - jnp.arange(RET_HEADS, dtype=F32)))
    idx = jnp.arange(RET_CHUNK, dtype=F32)
    q_dec = jnp.exp(lg[:, None] * idx)
    k_dec = jnp.exp(-lg[:, None] * idx) * RET_HEAD_DIM ** -0.5
    dec = jnp.concatenate([q_dec, k_dec], axis=0)[..., None]
    dec = jnp.broadcast_to(dec, (2 * RET_HEADS, RET_CHUNK, LANES)).astype(F32)
    chunk_dec = jnp.exp(lg * RET_CHUNK).astype(F32)
    return dec, chunk_dec


def kernel(x, positions, w_in, w_out, w_up, w_down, conv_w, conv_b, attn_sinks,
           pre_mix_norm, post_mix_norm, attn_out_norm, ret_out_norm,
           pre_ffn_norm, post_ffn_norm):
    batch, seq, d = x.shape
    m = batch * seq
    depth = w_in.shape[0]
    xf = x.reshape(m, d)
    pos_b = jnp.broadcast_to(positions.reshape(m, 1).astype(F32), (m, LANES))
    tables = _rope_tables(pos_b)
    dec, cdec = _retention_decay_tables()
    pre_mix = pre_mix_norm[:, None]
    post_mix = post_mix_norm[:, None]
    pre_ffn = pre_ffn_norm[:, None]
    post_ffn = post_ffn_norm[:, None]
    anorm = attn_out_norm[:, None]
    rnorm = ret_out_norm.reshape(depth, 1, D_RET)
    conv_b3 = conv_b[:, None]
    for l in range(depth):
        proj = _in_proj(xf, pre_mix, w_in, l, tables, dec)
        mix = _mixer(proj, attn_sinks[l], cdec, anorm, rnorm, l, batch)
        xf = _matmul_norm_res(mix, w_out, xf, post_mix, l)
        f = _ffn_up(xf, pre_ffn, w_up, conv_w, conv_b3, l, seq)
        xf = _matmul_norm_res(f, w_down, xf, post_ffn, l)
    return xf.reshape(batch, seq, d)
```

```python
import functools

import jax
import jax.numpy as jnp
from jax import lax
from jax.experimental import pallas as pl
from jax.experimental.pallas import tpu as pltpu

F32 = jnp.float32
BF16 = jnp.bfloat16

D_MODEL = 2048
D_ATTN = 1024
HEAD_DIM = 64
N_Q_HEADS = 16
N_KV_HEADS = 4
Q_PER_KV = N_Q_HEADS // N_KV_HEADS
BLOCK = 128
ROPE_DIM = 16
ROPE_THETA = 500000.0
D_RET = 1024
RET_HEADS = 4
RET_HEAD_DIM = 256
RET_THETA = 10000.0
D_KV = N_KV_HEADS * HEAD_DIM
D_IN = D_ATTN + 2 * D_KV + 4 * D_RET
D_FF = 5632
CONV_W = 3
EPS = 1e-6
NEG_INF = -1e30
LOG2E = 1.4426950408889634

LANES = 128
SUBLANES = 8
BF16_ROWS = 16
W_CHUNK_BYTES = 4 * 1024 * 1024
VMEM_LIMIT = 56 * 1024 * 1024
VMEM_BUDGET = 44 * 1024 * 1024

TM_IN = 1024
TN_IN = 512
TM_OUT = 256
NORM_ROWS = 128
MIX_ROWS = 1024
RET_CHUNK = 256
assert MIX_ROWS % RET_CHUNK == 0 and TM_OUT % RET_CHUNK == 0
GELU_C1 = 0.7978845608028654
GELU_C2 = GELU_C1 * 0.044715

Q_TILES = D_ATTN // TN_IN
KV_TILE = Q_TILES
RET_TILE0 = (D_ATTN + 2 * D_KV) // TN_IN
ROPE_R_TILES = (RET_TILE0, RET_TILE0 + 2 * D_RET // TN_IN)
HEADS_PER_TILE = TN_IN // RET_HEAD_DIM
assert 2 * D_KV == TN_IN and (D_ATTN + 2 * D_KV) % TN_IN == 0


def _rms_rows(xf, gain):
    ms = jnp.mean(xf * xf, axis=-1, keepdims=True)
    return xf * lax.rsqrt(ms + EPS) * gain


def _norm_into(x_ref, g_ref, h_ref):
    gain = g_ref[...]

    def body(r, carry):
        rows = pl.ds(pl.multiple_of(r * NORM_ROWS, NORM_ROWS), NORM_ROWS)
        h_ref[rows, :] = _rms_rows(x_ref[rows, :], gain).astype(h_ref.dtype)
        return carry

    lax.fori_loop(0, x_ref.shape[0] // NORM_ROWS, body, 0)


def _rope_table_kernel(pos_ref, inv_r_ref, inv_a_ref, rot_ref, sa_ref, sb_ref,
                       rc_ref, rs_ref, ac_ref, asa_ref, asb_ref):
    pos = pos_ref[...]
    ang_r = pos * inv_r_ref[...]
    rc_ref[...] = jnp.cos(ang_r)
    rs_ref[...] = jnp.sin(ang_r)
    ang_a = pos * inv_a_ref[...]
    ca = jnp.cos(ang_a)
    sa = jnp.sin(ang_a)
    ac_ref[...] = jnp.where(rot_ref[...] > 0.5, ca, 1.0)
    asa_ref[...] = sa * sa_ref[...]
    asb_ref[...] = sa * sb_ref[...]


def _rope_tables(pos_b):
    m = pos_b.shape[0]
    half_r = RET_HEAD_DIM // 2
    inv_r = jnp.power(F32(RET_THETA), -jnp.arange(half_r, dtype=F32) / half_r)
    half_a = ROPE_DIM // 2
    inv_a8 = jnp.power(F32(ROPE_THETA), -jnp.arange(half_a, dtype=F32) / half_a)
    lane = jnp.arange(LANES) % HEAD_DIM
    inv_a = jnp.where(lane < ROPE_DIM, inv_a8[lane % half_a], 0.0).astype(F32)
    rot = (lane < ROPE_DIM).astype(F32)
    sa = jnp.where(lane < half_a, -1.0, 0.0).astype(F32)
    sb = jnp.where((lane >= half_a) & (lane < ROPE_DIM), 1.0, 0.0).astype(F32)
    rows = 2048
    row_spec = pl.BlockSpec((rows, LANES), lambda i: (i, 0))
    lane_spec = pl.BlockSpec((1, LANES), lambda i: (0, 0))
    return pl.pallas_call(
        _rope_table_kernel,
        grid=(m // rows,),
        in_specs=[row_spec] + [lane_spec] * 5,
        out_specs=[row_spec] * 5,
        out_shape=[jax.ShapeDtypeStruct((m, LANES), F32)] * 5,
        compiler_params=pltpu.CompilerParams(
            dimension_semantics=("parallel",), vmem_limit_bytes=VMEM_LIMIT),
        name="rope_tables",
    )(pos_b, inv_r[None], inv_a[None], rot[None], sa[None], sb[None])


def _in_proj_kernel(n_load, x_ref, g_ref, w_ref, ac_ref, asa_ref, asb_ref, rc_ref, rs_ref,
                    dec_ref, o_ref, wb_ref):
    s = pl.program_id(0)

    @pl.when(s < n_load)
    def _():
        _load_weight_chunk(s, w_ref, wb_ref)

    @pl.when(s >= n_load)
    def _():
        _in_proj_tile(x_ref, g_ref, wb_ref, ac_ref, asa_ref, asb_ref, rc_ref, rs_ref, dec_ref, o_ref)


def _in_proj_tile(x_ref, g_ref, w_ref, ac_ref, asa_ref, asb_ref, rc_ref, rs_ref, dec_ref, o_ref):
    tm = x_ref.shape[0]
    h = _rms_rows(x_ref[...], g_ref[...]).astype(BF16)

    def attn_rope(acc, base, n_cols, scale):
        cos, sin_a, sin_b = ac_ref[...], asa_ref[...], asb_ref[...]
        for c in range(n_cols):
            xc = acc[:, c * LANES:(c + 1) * LANES]
            y = (xc * cos + pltpu.roll(xc, LANES - ROPE_DIM // 2, 1) * sin_a
                 + pltpu.roll(xc, ROPE_DIM // 2, 1) * sin_b)
            if scale is not None:
                y = y * scale
            o_ref[:, base + c * LANES:base + (c + 1) * LANES] = y.astype(o_ref.dtype)

    for t in range(w_ref.shape[1] // TN_IN):
        base = t * TN_IN
        acc = jnp.dot(h, w_ref[:, base:base + TN_IN], preferred_element_type=F32)
        if t < Q_TILES:
            attn_rope(acc, base, TN_IN // LANES, HEAD_DIM ** -0.5 * LOG2E)
        elif t == KV_TILE:
            attn_rope(acc, base, D_KV // LANES, None)
            o_ref[:, base + D_KV:base + TN_IN] = acc[:, D_KV:].astype(o_ref.dtype)
        elif t < ROPE_R_TILES[1]:
            half = RET_HEAD_DIM // 2
            cos, sin = rc_ref[...], rs_ref[...]
            for hh in range(HEADS_PER_TILE):
                head = (t - ROPE_R_TILES[0]) * HEADS_PER_TILE + hh
                dec = jnp.concatenate([dec_ref[head]] * (tm // RET_CHUNK), axis=0)
                lo = hh * RET_HEAD_DIM
                x1 = acc[:, lo:lo + half]
                x2 = acc[:, lo + half:lo + 2 * half]
                o_ref[:, base + lo:base + lo + half] = (
                    (x1 * cos - x2 * sin) * dec).astype(o_ref.dtype)
                o_ref[:, base + lo + half:base + lo + 2 * half] = (
                    (x2 * cos + x1 * sin) * dec).astype(o_ref.dtype)
        else:
            o_ref[:, base:base + TN_IN] = acc.astype(o_ref.dtype)


def _in_proj(x, gain, w, layer, tables, dec):
    m, d = x.shape
    n = w.shape[2]
    rc, rs, ac, asa, asb = tables
    chunk = _weight_chunk_rows(d, n)
    n_load = d // chunk

    def tile(s):
        return jnp.maximum(s - n_load, 0), 0

    tab_spec = pl.BlockSpec((TM_OUT, LANES), tile)
    return pl.pallas_call(
        functools.partial(_in_proj_kernel, n_load),
        grid=(n_load + m // TM_OUT,),
        in_specs=[pl.BlockSpec((TM_OUT, d), tile),
                  pl.BlockSpec((None, 1, d), lambda s: (layer, 0, 0)),
                  pl.BlockSpec((None, chunk, n),
                               lambda s: (layer, jnp.minimum(s, n_load - 1), 0)),
                  tab_spec, tab_spec, tab_spec, tab_spec, tab_spec,
                  pl.BlockSpec(dec.shape, lambda s: (0, 0, 0))],
        out_specs=pl.BlockSpec((TM_OUT, n), tile),
        out_shape=jax.ShapeDtypeStruct((m, n), BF16),
        scratch_shapes=[pltpu.VMEM((d, n), BF16)],
        compiler_params=pltpu.CompilerParams(
            dimension_semantics=("arbitrary",), vmem_limit_bytes=VMEM_LIMIT),
        name="norm_in_proj_rope",
    )(x, gain, w, ac, asa, asb, rc, rs, dec)


def _mixer_kernel(qa_ref, kc_ref, kp_ref, vc_ref, vp_ref,
                  qr0_ref, qr1_ref, kr0_ref, kr1_ref, vr0_ref, vr1_ref, gr0_ref, gr1_ref,
                  sink_ref, cdec_ref, anorm_ref, rnorm_ref,
                  o_ref, state_ref, ya_ref):
    n = pl.program_id(1)
    qr_refs, kr_refs = (qr0_ref, qr1_ref), (kr0_ref, kr1_ref)
    vr_refs, gr_refs = (vr0_ref, vr1_ref), (gr0_ref, gr1_ref)

    @pl.when(n == 0)
    def _():
        state_ref[...] = jnp.zeros_like(state_ref)

    lane_b = lax.broadcasted_iota(jnp.int32, (BLOCK, LANES), 1)
    low_b = lane_b < HEAD_DIM
    ones_lo = jnp.where(low_b, 1.0, 0.0).astype(BF16)
    ones_hi = jnp.where(low_b, 0.0, 1.0).astype(BF16)

    def placed(x_b):
        x = x_b.astype(F32)
        xr = pltpu.roll(x, HEAD_DIM, 1)
        views = ((jnp.where(low_b, x, 0.0), jnp.where(low_b, 0.0, xr)),
                 (jnp.where(low_b, xr, 0.0), jnp.where(low_b, 0.0, x)))
        return [tuple(t.astype(BF16) for t in pair) for pair in views]

    def kv_matrices(k_ref, v_ref, rows):
        ks, vs = [], []
        for c in range(D_KV // LANES):
            cols = slice(c * LANES, (c + 1) * LANES)
            for k2, (lo, hi) in zip(placed(k_ref[rows, cols]), placed(v_ref[rows, cols])):
                ks.append(jnp.concatenate(k2, axis=0))
                vs.append(jnp.concatenate([jnp.concatenate([lo, ones_lo], axis=1),
                                           jnp.concatenate([hi, ones_hi], axis=1)], axis=0))
        return ks, vs

    n_sub = qa_ref.shape[0] // BLOCK
    blocks = [kv_matrices(kp_ref, vp_ref, slice(0, BLOCK))]
    blocks += [kv_matrices(kc_ref, vc_ref, slice(sb * BLOCK, (sb + 1) * BLOCK))
               for sb in range(n_sub)]

    qi = lax.broadcasted_iota(jnp.int32, (BLOCK, 2 * BLOCK), 0)
    kj = lax.broadcasted_iota(jnp.int32, (BLOCK, 2 * BLOCK), 1) & (BLOCK - 1)
    in_cur = kj <= qi
    no_prev = jnp.where(n > 0, 0.0, NEG_INF)
    qk_dims = (((1,), (1,)), ((), ()))

    for sb in range(n_sub):
        rows = slice(sb * BLOCK, (sb + 1) * BLOCK)
        (k_prev, v_prev), (k_cur, v_cur) = blocks[sb], blocks[sb + 1]
        for c in range(D_ATTN // LANES):
            g = (2 * c) // Q_PER_KV
            qc = qa_ref[rows, c * LANES:(c + 1) * LANES]
            s_cur = lax.dot_general(qc, k_cur[g], qk_dims, preferred_element_type=F32)
            s_prev = lax.dot_general(qc, k_prev[g], qk_dims, preferred_element_type=F32)
            if sb == 0:
                s_prev = s_prev + no_prev
            s = jnp.where(in_cur, s_cur, s_prev)
            ps, sink_terms = [], []
            for hh in range(2):
                sh = s[:, hh * BLOCK:(hh + 1) * BLOCK]
                sink = sink_ref[2 * c + hh] * LOG2E
                mx = jnp.maximum(jnp.max(sh, axis=-1, keepdims=True), sink)
                ps.append(jnp.exp2(sh - mx))
                sink_terms.append(jnp.exp2(sink - mx))
            p = jnp.concatenate(ps, axis=1)
            acc = (jnp.dot(jnp.where(in_cur, p, 0.0).astype(BF16), v_cur[g],
                           preferred_element_type=F32)
                   + jnp.dot(jnp.where(in_cur, 0.0, p).astype(BF16), v_prev[g],
                             preferred_element_type=F32))
            denom = acc[:, LANES:] + jnp.where(low_b, sink_terms[0], sink_terms[1])
            ya_ref[rows, c * LANES:(c + 1) * LANES] = acc[:, :LANES] / denom

    o_ref[:, :D_ATTN] = _rms_rows(ya_ref[...], anorm_ref[...]).astype(o_ref.dtype)

    ri = lax.broadcasted_iota(jnp.int32, (RET_CHUNK, RET_CHUNK), 0)
    rj = lax.broadcasted_iota(jnp.int32, (RET_CHUNK, RET_CHUNK), 1)
    causal = ri >= rj
    for h in range(RET_HEADS):
        cols = slice(h * RET_HEAD_DIM, (h + 1) * RET_HEAD_DIM)
        pair, side = divmod(h, HEADS_PER_TILE)
        pcols = slice(side * RET_HEAD_DIM, (side + 1) * RET_HEAD_DIM)
        state = state_ref[h]
        for r0 in range(0, qa_ref.shape[0], RET_CHUNK):
            rows = slice(r0, r0 + RET_CHUNK)
            q = qr_refs[pair][rows, pcols]
            k = kr_refs[pair][rows, pcols]
            v = vr_refs[pair][rows, pcols]
            sc = lax.dot_general(q, k, qk_dims, preferred_element_type=F32)
            sc = jnp.where(causal, sc, 0.0).astype(BF16)
            y = (jnp.dot(sc, v, preferred_element_type=F32)
                 + jnp.dot(q, state.astype(BF16), preferred_element_type=F32))
            kv = lax.dot_general(k, v, (((0,), (0,)), ((), ())), preferred_element_type=F32)
            state = (state + kv) * cdec_ref[h]
            y = _rms_rows(y, rnorm_ref[:, cols])
            gate = gr_refs[pair][rows, pcols].astype(F32)
            o_ref[rows, D_ATTN + h * RET_HEAD_DIM:D_ATTN + (h + 1) * RET_HEAD_DIM] = (
                y * (gate * jax.nn.sigmoid(gate))).astype(o_ref.dtype)
        state_ref[h] = state


def _mixer(proj, sinks, cdec, anorm, rnorm, layer, batch):
    m = proj.shape[0]
    steps = m // batch // MIX_ROWS
    sub = MIX_ROWS // BLOCK

    def rows(width, col):
        return pl.BlockSpec((MIX_ROWS, width), lambda b, n: (b * steps + n, col))

    def prev_rows(width, col):
        return pl.BlockSpec(
            (BLOCK, width), lambda b, n: ((b * steps + n) * sub - jnp.minimum(n, 1), col))

    k_col = D_ATTN // D_KV
    ret_specs = [rows(TN_IN, RET_TILE0 + t) for t in range(4 * D_RET // TN_IN)]
    smem = pl.BlockSpec(memory_space=pltpu.SMEM)
    return pl.pallas_call(
        _mixer_kernel,
        grid=(batch, steps),
        in_specs=[rows(D_ATTN, 0),
                  rows(D_KV, k_col), prev_rows(D_KV, k_col),
                  rows(D_KV, k_col + 1), prev_rows(D_KV, k_col + 1),
                  *ret_specs,
                  smem, smem,
                  pl.BlockSpec((None, 1, D_ATTN), lambda b, n: (layer, 0, 0)),
                  pl.BlockSpec((None, 1, D_RET), lambda b, n: (layer, 0, 0))],
        out_specs=rows(D_ATTN + D_RET, 0),
        out_shape=jax.ShapeDtypeStruct((m, D_ATTN + D_RET), BF16),
        scratch_shapes=[pltpu.VMEM((RET_HEADS, RET_HEAD_DIM, RET_HEAD_DIM), F32),
                        pltpu.VMEM((MIX_ROWS, D_ATTN), F32)],
        compiler_params=pltpu.CompilerParams(
            dimension_semantics=("parallel", "arbitrary"),
            vmem_limit_bytes=VMEM_LIMIT),
        name="token_mixer",
    )(*([proj] * (5 + len(ret_specs))), sinks, cdec, anorm, rnorm)


def _weight_chunk_rows(k, n):
    rows = BF16_ROWS
    while rows * 2 * n * 4 <= W_CHUNK_BYTES and k % (rows * 2) == 0:
        rows *= 2
    return rows


def _load_weight_chunk(step, w_ref, wb_ref):
    ck = w_ref.shape[0]
    rows = pl.ds(pl.multiple_of(step * ck, ck), ck)
    wb_ref[rows, :] = w_ref[...].astype(wb_ref.dtype)


def _matmul_norm_res_kernel(n_load, a_ref, w_ref, x_ref, g_ref, o_ref, wb_ref):
    s = pl.program_id(0)

    @pl.when(s < n_load)
    def _():
        _load_weight_chunk(s, w_ref, wb_ref)

    @pl.when(s >= n_load)
    def _():
        for r0 in range(0, a_ref.shape[0], TM_OUT):
            rows = slice(r0, r0 + TM_OUT)
            y = jnp.dot(a_ref[rows, :], wb_ref[...], preferred_element_type=F32)
            o_ref[rows, :] = x_ref[rows, :] + _rms_rows(y, g_ref[...])


def _matmul_norm_res(a, w, x, gain, layer):
    m, k = a.shape
    d = w.shape[2]
    chunk = _weight_chunk_rows(k, d)
    n_load = k // chunk
    resident = k * d * 2 + 2 * chunk * d * 4
    per_row = 2 * (k * 2 + d * 4 + d * 4)
    tm = TM_OUT
    while resident + 2 * tm * per_row <= VMEM_BUDGET and m % (2 * tm) == 0 and tm < TM_IN:
        tm *= 2

    def tile(s):
        return jnp.maximum(s - n_load, 0), 0

    return pl.pallas_call(
        functools.partial(_matmul_norm_res_kernel, n_load),
        grid=(n_load + m // tm,),
        in_specs=[pl.BlockSpec((tm, k), tile),
                  pl.BlockSpec((None, chunk, d),
                               lambda s: (layer, jnp.minimum(s, n_load - 1), 0)),
                  pl.BlockSpec((tm, d), tile),
                  pl.BlockSpec((None, 1, d), lambda s: (layer, 0, 0))],
        out_specs=pl.BlockSpec((tm, d), tile),
        out_shape=jax.ShapeDtypeStruct((m, d), F32),
        scratch_shapes=[pltpu.VMEM((k, d), BF16)],
        compiler_params=pltpu.CompilerParams(
            dimension_semantics=("arbitrary",), vmem_limit_bytes=VMEM_LIMIT),
        name="proj_norm_residual",
    )(a, w, x, gain)


def _ffn_up_kernel(tiles_per_seq, x_ref, g_ref, wa_ref, wg_ref, cwa_ref, cwg_ref,
                   cba_ref, cbg_ref, o_ref, h_ref, ua_ref, ug_ref, carry_ref):
    i = pl.program_id(0)
    j = pl.program_id(1)
    tm = x_ref.shape[0]

    @pl.when(j == 0)
    def _():
        _norm_into(x_ref, g_ref, h_ref)

    @pl.when((i == 0) & (j == 0))
    def _():
        carry_ref[...] = jnp.zeros_like(carry_ref)

    starts_sequence = (i % tiles_per_seq) == 0
    h = h_ref[...]

    def conv(u_ref, w_ref, cw, cb, slot):
        u = jnp.dot(h, w_ref[...].astype(BF16), preferred_element_type=F32)
        halo = jnp.where(starts_sequence, 0.0, carry_ref[slot, j])
        carry_ref[slot, j] = u[tm - SUBLANES:, :]
        ext = jnp.concatenate([halo, u], axis=0)
        d1 = pltpu.roll(ext, 1, 0)[SUBLANES:, :]
        d2 = pltpu.roll(ext, 2, 0)[SUBLANES:, :]
        return cb + d2 * cw[0:1] + d1 * cw[1:2] + u * cw[2:3]

    a = conv(ua_ref, wa_ref, cwa_ref[...], cba_ref[...], 0)
    half_gate = conv(ug_ref, wg_ref, cwg_ref[...] * 0.5, cbg_ref[...] * 0.5, 1)
    t = jnp.tanh(a * (GELU_C1 + GELU_C2 * (a * a)))
    o_ref[...] = ((a + a * t) * half_gate).astype(o_ref.dtype)


def _ffn_up(x, gain, w_up, conv_w, conv_b, layer, seq):
    m, d = x.shape
    nj = D_FF // TN_IN
    kern = functools.partial(_ffn_up_kernel, seq // TM_IN)
    return pl.pallas_call(
        kern,
        grid=(m // TM_IN, nj),
        in_specs=[pl.BlockSpec((TM_IN, d), lambda i, j: (i, 0)),
                  pl.BlockSpec((None, 1, d), lambda i, j: (layer, 0, 0)),
                  pl.BlockSpec((None, d, TN_IN), lambda i, j: (layer, 0, j)),
                  pl.BlockSpec((None, d, TN_IN), lambda i, j: (layer, 0, j + nj)),
                  pl.BlockSpec((None, CONV_W, TN_IN), lambda i, j: (layer, 0, j)),
                  pl.BlockSpec((None, CONV_W, TN_IN), lambda i, j: (layer, 0, j + nj)),
                  pl.BlockSpec((None, 1, TN_IN), lambda i, j: (layer, 0, j)),
                  pl.BlockSpec((None, 1, TN_IN), lambda i, j: (layer, 0, j + nj))],
        out_specs=pl.BlockSpec((TM_IN, TN_IN), lambda i, j: (i, j)),
        out_shape=jax.ShapeDtypeStruct((m, D_FF), BF16),
        scratch_shapes=[pltpu.VMEM((TM_IN, d), BF16),
                        pltpu.VMEM((TM_IN + SUBLANES, TN_IN), F32),
                        pltpu.VMEM((TM_IN + SUBLANES, TN_IN), F32),
                        pltpu.VMEM((2, nj, SUBLANES, TN_IN), F32)],
        compiler_params=pltpu.CompilerParams(
            dimension_semantics=("arbitrary", "arbitrary"),
            vmem_limit_bytes=VMEM_LIMIT),
        name="ffn_up_conv_gate",
    )(x, gain, w_up, w_up, conv_w, conv_w, conv_b, conv_b)


def _retention_decay_tables():
    lg = jnp.log(1.0 - jnp.power(2.0, -5.0 - jnp.arange(RET_HEADS, dtype=F32)))
    idx = jnp.arange(RET_CHUNK, dtype=F32)
    q_dec = jnp.exp(lg[:, None] * idx)
    k_dec = jnp.exp(-lg[:, None] * idx) * RET_HEAD_DIM ** -0.5
    dec = jnp.concatenate([q_dec, k_dec], axis=0)[..., None]
    dec = jnp.broadcast_to(dec, (2 * RET_HEADS, RET_CHUNK, LANES)).astype(F32)
    chunk_dec = jnp.exp(lg * RET_CHUNK).astype(F32)
    return dec, chunk_dec


def kernel(x, positions, w_in, w_out, w_up, w_down, conv_w, conv_b, attn_sinks,
           pre_mix_norm, post_mix_norm, attn_out_norm, ret_out_norm,
           pre_ffn_norm, post_ffn_norm):
    batch, seq, d = x.shape
    m = batch * seq
    depth = w_in.shape[0]
    xf = x.reshape(m, d)
    pos_b = jnp.broadcast_to(positions.reshape(m, 1).astype(F32), (m, LANES))
    tables = _rope_tables(pos_b)
    dec, cdec = _retention_decay_tables()
    pre_mix = pre_mix_norm[:, None]
    post_mix = post_mix_norm[:, None]
    pre_ffn = pre_ffn_norm[:, None]
    post_ffn = post_ffn_norm[:, None]
    anorm = attn_out_norm[:, None]
    rnorm = ret_out_norm.reshape(depth, 1, D_RET)
    conv_b3 = conv_b[:, None]
    for l in range(depth):
        proj = _in_proj(xf, pre_mix, w_in, l, tables, dec)
        mix = _mixer(proj, attn_sinks[l], cdec, anorm, rnorm, l, batch)
        xf = _matmul_norm_res(mix, w_out, xf, post_mix, l)
        f = _ffn_up(xf, pre_ffn, w_up, conv_w, conv_b3, l, seq)
        xf = _matmul_norm_res(f, w_down, xf, post_ffn, l)
    return xf.reshape(batch, seq, d)
```

```python
import functools

import jax
import jax.numpy as jnp
from jax import lax
from jax.experimental import pallas as pl
from jax.experimental.pallas import tpu as pltpu

F32 = jnp.float32
BF16 = jnp.bfloat16

D_MODEL = 2048
D_ATTN = 1024
HEAD_DIM = 64
N_Q_HEADS = 16
N_KV_HEADS = 4
Q_PER_KV = N_Q_HEADS // N_KV_HEADS
BLOCK = 128
ROPE_DIM = 16
ROPE_THETA = 500000.0
D_RET = 1024
RET_HEADS = 4
RET_HEAD_DIM = 256
RET_THETA = 10000.0
D_KV = N_KV_HEADS * HEAD_DIM
D_IN = D_ATTN + 2 * D_KV + 4 * D_RET
D_FF = 5632
CONV_W = 3
EPS = 1e-6
NEG_INF = -1e30
LOG2E = 1.4426950408889634

LANES = 128
SUBLANES = 8
BF16_ROWS = 16
W_CHUNK_BYTES = 4 * 1024 * 1024
VMEM_LIMIT = 56 * 1024 * 1024
VMEM_BUDGET = 44 * 1024 * 1024

TM_IN = 1024
TN_IN = 512
TM_OUT = 256
NORM_ROWS = 128
MIX_ROWS = 1024
RET_CHUNK = 256
assert MIX_ROWS % RET_CHUNK == 0 and TM_OUT % RET_CHUNK == 0
GELU_C1 = 0.7978845608028654
GELU_C2 = GELU_C1 * 0.044715

Q_TILES = D_ATTN // TN_IN
KV_TILE = Q_TILES
RET_TILE0 = (D_ATTN + 2 * D_KV) // TN_IN
ROPE_R_TILES = (RET_TILE0, RET_TILE0 + 2 * D_RET // TN_IN)
HEADS_PER_TILE = TN_IN // RET_HEAD_DIM
assert 2 * D_KV == TN_IN and (D_ATTN + 2 * D_KV) % TN_IN == 0


def _rms_rows(xf, gain):
    ms = jnp.mean(xf * xf, axis=-1, keepdims=True)
    return xf * lax.rsqrt(ms + EPS) * gain


def _norm_into(x_ref, g_ref, h_ref):
    gain = g_ref[...]

    def body(r, carry):
        rows = pl.ds(pl.multiple_of(r * NORM_ROWS, NORM_ROWS), NORM_ROWS)
        h_ref[rows, :] = _rms_rows(x_ref[rows, :], gain).astype(h_ref.dtype)
        return carry

    lax.fori_loop(0, x_ref.shape[0] // NORM_ROWS, body, 0)


def _rope_table_kernel(pos_ref, inv_r_ref, inv_a_ref, rot_ref, sa_ref, sb_ref,
                       rc_ref, rs_ref, ac_ref, asa_ref, asb_ref):
    pos = pos_ref[...]
    ang_r = pos * inv_r_ref[...]
    rc_ref[...] = jnp.cos(ang_r)
    rs_ref[...] = jnp.sin(ang_r)
    ang_a = pos * inv_a_ref[...]
    ca = jnp.cos(ang_a)
    sa = jnp.sin(ang_a)
    ac_ref[...] = jnp.where(rot_ref[...] > 0.5, ca, 1.0)
    asa_ref[...] = sa * sa_ref[...]
    asb_ref[...] = sa * sb_ref[...]


def _rope_tables(pos_b):
    m = pos_b.shape[0]
    half_r = RET_HEAD_DIM // 2
    inv_r = jnp.power(F32(RET_THETA), -jnp.arange(half_r, dtype=F32) / half_r)
    half_a = ROPE_DIM // 2
    inv_a8 = jnp.power(F32(ROPE_THETA), -jnp.arange(half_a, dtype=F32) / half_a)
    lane = jnp.arange(LANES) % HEAD_DIM
    inv_a = jnp.where(lane < ROPE_DIM, inv_a8[lane % half_a], 0.0).astype(F32)
    rot = (lane < ROPE_DIM).astype(F32)
    sa = jnp.where(lane < half_a, -1.0, 0.0).astype(F32)
    sb = jnp.where((lane >= half_a) & (lane < ROPE_DIM), 1.0, 0.0).astype(F32)
    rows = 2048
    row_spec = pl.BlockSpec((rows, LANES), lambda i: (i, 0))
    lane_spec = pl.BlockSpec((1, LANES), lambda i: (0, 0))
    return pl.pallas_call(
        _rope_table_kernel,
        grid=(m // rows,),
        in_specs=[row_spec] + [lane_spec] * 5,
        out_specs=[row_spec] * 5,
        out_shape=[jax.ShapeDtypeStruct((m, LANES), F32)] * 5,
        compiler_params=pltpu.CompilerParams(
            dimension_semantics=("parallel",), vmem_limit_bytes=VMEM_LIMIT),
        name="rope_tables",
    )(pos_b, inv_r[None], inv_a[None], rot[None], sa[None], sb[None])


def _in_proj_kernel(n_load, x_ref, g_ref, w_ref, ac_ref, asa_ref, asb_ref, rc_ref, rs_ref,
                    dec_ref, o_ref, wb_ref):
    s = pl.program_id(0)

    @pl.when(s < n_load)
    def _():
        _load_weight_chunk(s, w_ref, wb_ref)

    @pl.when(s >= n_load)
    def _():
        _in_proj_tile(x_ref, g_ref, wb_ref, ac_ref, asa_ref, asb_ref, rc_ref, rs_ref, dec_ref, o_ref)


def _in_proj_tile(x_ref, g_ref, w_ref, ac_ref, asa_ref, asb_ref, rc_ref, rs_ref, dec_ref, o_ref):
    tm = x_ref.shape[0]
    h = _rms_rows(x_ref[...], g_ref[...]).astype(BF16)

    def attn_rope(acc, base, n_cols, scale):
        cos, sin_a, sin_b = ac_ref[...], asa_ref[...], asb_ref[...]
        for c in range(n_cols):
            xc = acc[:, c * LANES:(c + 1) * LANES]
            y = (xc * cos + pltpu.roll(xc, LANES - ROPE_DIM // 2, 1) * sin_a
                 + pltpu.roll(xc, ROPE_DIM // 2, 1) * sin_b)
            if scale is not None:
                y = y * scale
            o_ref[:, base + c * LANES:base + (c + 1) * LANES] = y.astype(o_ref.dtype)

    for t in range(w_ref.shape[1] // TN_IN):
        base = t * TN_IN
        acc = jnp.dot(h, w_ref[:, base:base + TN_IN], preferred_element_type=F32)
        if t < Q_TILES:
            attn_rope(acc, base, TN_IN // LANES, HEAD_DIM ** -0.5 * LOG2E)
        elif t == KV_TILE:
            attn_rope(acc, base, D_KV // LANES, None)
            o_ref[:, base + D_KV:base + TN_IN] = acc[:, D_KV:].astype(o_ref.dtype)
        elif t < ROPE_R_TILES[1]:
            half = RET_HEAD_DIM // 2
            cos, sin = rc_ref[...], rs_ref[...]
            for hh in range(HEADS_PER_TILE):
                head = (t - ROPE_R_TILES[0]) * HEADS_PER_TILE + hh
                dec = jnp.concatenate([dec_ref[head]] * (tm // RET_CHUNK), axis=0)
                lo = hh * RET_HEAD_DIM
                x1 = acc[:, lo:lo + half]
                x2 = acc[:, lo + half:lo + 2 * half]
                o_ref[:, base + lo:base + lo + half] = (
                    (x1 * cos - x2 * sin) * dec).astype(o_ref.dtype)
                o_ref[:, base + lo + half:base + lo + 2 * half] = (
                    (x2 * cos + x1 * sin) * dec).astype(o_ref.dtype)
        else:
            o_ref[:, base:base + TN_IN] = acc.astype(o_ref.dtype)


def _in_proj(x, gain, w, layer, tables, dec):
    m, d = x.shape
    n = w.shape[2]
    rc, rs, ac, asa, asb = tables
    chunk = _weight_chunk_rows(d, n)
    n_load = d // chunk

    def tile(s):
        return jnp.maximum(s - n_load, 0), 0

    tab_spec = pl.BlockSpec((TM_OUT, LANES), tile)
    return pl.pallas_call(
        functools.partial(_in_proj_kernel, n_load),
        grid=(n_load + m // TM_OUT,),
        in_specs=[pl.BlockSpec((TM_OUT, d), tile),
                  pl.BlockSpec((None, 1, d), lambda s: (layer, 0, 0)),
                  pl.BlockSpec((None, chunk, n),
                               lambda s: (layer, jnp.minimum(s, n_load - 1), 0)),
                  tab_spec, tab_spec, tab_spec, tab_spec, tab_spec,
                  pl.BlockSpec(dec.shape, lambda s: (0, 0, 0))],
        out_specs=pl.BlockSpec((TM_OUT, n), tile),
        out_shape=jax.ShapeDtypeStruct((m, n), BF16),
        scratch_shapes=[pltpu.VMEM((d, n), BF16)],
        compiler_params=pltpu.CompilerParams(
            dimension_semantics=("arbitrary",), vmem_limit_bytes=VMEM_LIMIT),
        name="norm_in_proj_rope",
    )(x, gain, w, ac, asa, asb, rc, rs, dec)


def _mixer_kernel(qa_ref, kc_ref, kp_ref, vc_ref, vp_ref,
                  qr0_ref, qr1_ref, kr0_ref, kr1_ref, vr0_ref, vr1_ref, gr0_ref, gr1_ref,
                  sink_ref, cdec_ref, anorm_ref, rnorm_ref,
                  o_ref, state_ref, ya_ref):
    n = pl.program_id(1)
    qr_refs, kr_refs = (qr0_ref, qr1_ref), (kr0_ref, kr1_ref)
    vr_refs, gr_refs = (vr0_ref, vr1_ref), (gr0_ref, gr1_ref)

    @pl.when(n == 0)
    def _():
        state_ref[...] = jnp.zeros_like(state_ref)

    lane_b = lax.broadcasted_iota(jnp.int32, (BLOCK, LANES), 1)
    low_b = lane_b < HEAD_DIM
    ones_lo = jnp.where(low_b, 1.0, 0.0).astype(BF16)
    ones_hi = jnp.where(low_b, 0.0, 1.0).astype(BF16)

    def placed(x_b):
        x = x_b.astype(F32)
        xr = pltpu.roll(x, HEAD_DIM, 1)
        views = ((jnp.where(low_b, x, 0.0), jnp.where(low_b, 0.0, xr)),
                 (jnp.where(low_b, xr, 0.0), jnp.where(low_b, 0.0, x)))
        return [tuple(t.astype(BF16) for t in pair) for pair in views]

    def kv_matrices(k_ref, v_ref, rows):
        ks, vs = [], []
        for c in range(D_KV // LANES):
            cols = slice(c * LANES, (c + 1) * LANES)
            for k2, (lo, hi) in zip(placed(k_ref[rows, cols]), placed(v_ref[rows, cols])):
                ks.append(jnp.concatenate(k2, axis=0))
                vs.append(jnp.concatenate([jnp.concatenate([lo, ones_lo], axis=1),
                                           jnp.concatenate([hi, ones_hi], axis=1)], axis=0))
        return ks, vs

    n_sub = qa_ref.shape[0] // BLOCK
    blocks = [kv_matrices(kp_ref, vp_ref, slice(0, BLOCK))]
    blocks += [kv_matrices(kc_ref, vc_ref, slice(sb * BLOCK, (sb + 1) * BLOCK))
               for sb in range(n_sub)]

    qi = lax.broadcasted_iota(jnp.int32, (BLOCK, 2 * BLOCK), 0)
    kj = lax.broadcasted_iota(jnp.int32, (BLOCK, 2 * BLOCK), 1) & (BLOCK - 1)
    in_cur = kj <= qi
    no_prev = jnp.where(n > 0, 0.0, NEG_INF)
    qk_dims = (((1,), (1,)), ((), ()))

    for sb in range(n_sub):
        rows = slice(sb * BLOCK, (sb + 1) * BLOCK)
        (k_prev, v_prev), (k_cur, v_cur) = blocks[sb], blocks[sb + 1]
        for c in range(D_ATTN // LANES):
            g = (2 * c) // Q_PER_KV
            qc = qa_ref[rows, c * LANES:(c + 1) * LANES]
            s_cur = lax.dot_general(qc, k_cur[g], qk_dims, preferred_element_type=F32)
            s_prev = lax.dot_general(qc, k_prev[g], qk_dims, preferred_element_type=F32)
            if sb == 0:
                s_prev = s_prev + no_prev
            s = jnp.where(in_cur, s_cur, s_prev)
            ps, sink_terms = [], []
            for hh in range(2):
                sh = s[:, hh * BLOCK:(hh + 1) * BLOCK]
                sink = sink_ref[2 * c + hh] * LOG2E
                mx = jnp.maximum(jnp.max(sh, axis=-1, keepdims=True), sink)
                ps.append(jnp.exp2(sh - mx))
                sink_terms.append(jnp.exp2(sink - mx))
            p = jnp.concatenate(ps, axis=1)
            acc = (jnp.dot(jnp.where(in_cur, p, 0.0).astype(BF16), v_cur[g],
                           preferred_element_type=F32)
                   + jnp.dot(jnp.where(in_cur, 0.0, p).astype(BF16), v_prev[g],
                             preferred_element_type=F32))
            denom = acc[:, LANES:] + jnp.where(low_b, sink_terms[0], sink_terms[1])
            ya_ref[rows, c * LANES:(c + 1) * LANES] = acc[:, :LANES] / denom

    o_ref[:, :D_ATTN] = _rms_rows(ya_ref[...], anorm_ref[...]).astype(o_ref.dtype)

    ri = lax.broadcasted_iota(jnp.int32, (RET_CHUNK, RET_CHUNK), 0)
    rj = lax.broadcasted_iota(jnp.int32, (RET_CHUNK, RET_CHUNK), 1)
    causal = ri >= rj
    for h in range(RET_HEADS):
        cols = slice(h * RET_HEAD_DIM, (h + 1) * RET_HEAD_DIM)
        pair, side = divmod(h, HEADS_PER_TILE)
        pcols = slice(side * RET_HEAD_DIM, (side + 1) * RET_HEAD_DIM)
        state = state_ref[h]
        for r0 in range(0, qa_ref.shape[0], RET_CHUNK):
            rows = slice(r0, r0 + RET_CHUNK)
            q = qr_refs[pair][rows, pcols]
            k = kr_refs[pair][rows, pcols]
            v = vr_refs[pair][rows, pcols]
            sc = lax.dot_general(q, k, qk_dims, preferred_element_type=F32)
            sc = jnp.where(causal, sc, 0.0).astype(BF16)
            y = (jnp.dot(sc, v, preferred_element_type=F32)
                 + jnp.dot(q, state.astype(BF16), preferred_element_type=F32))
            kv = lax.dot_general(k, v, (((0,), (0,)), ((), ())), preferred_element_type=F32)
            state = (state + kv) * cdec_ref[h]
            y = _rms_rows(y, rnorm_ref[:, cols])
            gate = gr_refs[pair][rows, pcols].astype(F32)
            o_ref[rows, D_ATTN + h * RET_HEAD_DIM:D_ATTN + (h + 1) * RET_HEAD_DIM] = (
                y * (gate * jax.nn.sigmoid(gate))).astype(o_ref.dtype)
        state_ref[h] = state


def _mixer(proj, sinks, cdec, anorm, rnorm, layer, batch):
    m = proj.shape[0]
    steps = m // batch // MIX_ROWS
    sub = MIX_ROWS // BLOCK

    def rows(width, col):
        return pl.BlockSpec((MIX_ROWS, width), lambda b, n: (b * steps + n, col))

    def prev_rows(width, col):
        return pl.BlockSpec(
            (BLOCK, width), lambda b, n: ((b * steps + n) * sub - jnp.minimum(n, 1), col))

    k_col = D_ATTN // D_KV
    ret_specs = [rows(TN_IN, RET_TILE0 + t) for t in range(4 * D_RET // TN_IN)]
    smem = pl.BlockSpec(memory_space=pltpu.SMEM)
    return pl.pallas_call(
        _mixer_kernel,
        grid=(batch, steps),
        in_specs=[rows(D_ATTN, 0),
                  rows(D_KV, k_col), prev_rows(D_KV, k_col),
                  rows(D_KV, k_col + 1), prev_rows(D_KV, k_col + 1),
                  *ret_specs,
                  smem, smem,
                  pl.BlockSpec((None, 1, D_ATTN), lambda b, n: (layer, 0, 0)),
                  pl.BlockSpec((None, 1, D_RET), lambda b, n: (layer, 0, 0))],
        out_specs=rows(D_ATTN + D_RET, 0),
        out_shape=jax.ShapeDtypeStruct((m, D_ATTN + D_RET), BF16),
        scratch_shapes=[pltpu.VMEM((RET_HEADS, RET_HEAD_DIM, RET_HEAD_DIM), F32),
                        pltpu.VMEM((MIX_ROWS, D_ATTN), F32)],
        compiler_params=pltpu.CompilerParams(
            dimension_semantics=("parallel", "arbitrary"),
            vmem_limit_bytes=VMEM_LIMIT),
        name="token_mixer",
    )(*([proj] * (5 + len(ret_specs))), sinks, cdec, anorm, rnorm)


def _weight_chunk_rows(k, n):
    rows = BF16_ROWS
    while rows * 2 * n * 4 <= W_CHUNK_BYTES and k % (rows * 2) == 0:
        rows *= 2
    return rows


def _load_weight_chunk(step, w_ref, wb_ref):
    ck = w_ref.shape[0]
    rows = pl.ds(pl.multiple_of(step * ck, ck), ck)
    wb_ref[rows, :] = w_ref[...].astype(wb_ref.dtype)


def _matmul_norm_res_kernel(n_load, a_ref, w_ref, x_ref, g_ref, o_ref, wb_ref):
    s = pl.program_id(0)

    @pl.when(s < n_load)
    def _():
        _load_weight_chunk(s, w_ref, wb_ref)

    @pl.when(s >= n_load)
    def _():
        for r0 in range(0, a_ref.shape[0], TM_OUT):
            rows = slice(r0, r0 + TM_OUT)
            y = jnp.dot(a_ref[rows, :], wb_ref[...], preferred_element_type=F32)
            o_ref[rows, :] = x_ref[rows, :] + _rms_rows(y, g_ref[...])


def _matmul_norm_res(a, w, x, gain, layer):
    m, k = a.shape
    d = w.shape[2]
    chunk = _weight_chunk_rows(k, d)
    n_load = k // chunk
    resident = k * d * 2 + 2 * chunk * d * 4
    per_row = 2 * (k * 2 + d * 4 + d * 4)
    tm = TM_OUT
    while resident + 2 * tm * per_row <= VMEM_BUDGET and m % (2 * tm) == 0 and tm < TM_IN:
        tm *= 2

    def tile(s):
        return jnp.maximum(s - n_load, 0), 0

    return pl.pallas_call(
        functools.partial(_matmul_norm_res_kernel, n_load),
        grid=(n_load + m // tm,),
        in_specs=[pl.BlockSpec((tm, k), tile),
                  pl.BlockSpec((None, chunk, d),
                               lambda s: (layer, jnp.minimum(s, n_load - 1), 0)),
                  pl.BlockSpec((tm, d), tile),
                  pl.BlockSpec((None, 1, d), lambda s: (layer, 0, 0))],
        out_specs=pl.BlockSpec((tm, d), tile),
        out_shape=jax.ShapeDtypeStruct((m, d), F32),
        scratch_shapes=[pltpu.VMEM((k, d), BF16)],
        compiler_params=pltpu.CompilerParams(
            dimension_semantics=("arbitrary",), vmem_limit_bytes=VMEM_LIMIT),
        name="proj_norm_residual",
    )(a, w, x, gain)


def _ffn_up_kernel(tiles_per_seq, x_ref, g_ref, wa_ref, wg_ref, cwa_ref, cwg_ref,
                   cba_ref, cbg_ref, o_ref, h_ref, carry_ref):
    i = pl.program_id(0)
    j = pl.program_id(1)
    tm = x_ref.shape[0]

    @pl.when(j == 0)
    def _():
        _norm_into(x_ref, g_ref, h_ref)

    @pl.when((i == 0) & (j == 0))
    def _():
        carry_ref[...] = jnp.zeros_like(carry_ref)

    starts_sequence = (i % tiles_per_seq) == 0
    h = h_ref[...]

    def conv(w_ref, cw, cb, slot):
        u = jnp.dot(h, w_ref[...].astype(BF16), preferred_element_type=F32)
        halo = jnp.where(starts_sequence, 0.0, carry_ref[slot, j])
        carry_ref[slot, j] = u[tm - SUBLANES:, :]
        ext = jnp.concatenate([halo, u], axis=0)
        d1 = pltpu.roll(ext, 1, 0)[SUBLANES:, :]
        d2 = pltpu.roll(ext, 2, 0)[SUBLANES:, :]
        return cb + d2 * cw[0:1] + d1 * cw[1:2] + u * cw[2:3]

    a = conv(wa_ref, cwa_ref[...], cba_ref[...], 0)
    half_gate = conv(wg_ref, cwg_ref[...] * 0.5, cbg_ref[...] * 0.5, 1)
    t = jnp.tanh(a * (GELU_C1 + GELU_C2 * (a * a)))
    o_ref[...] = ((a + a * t) * half_gate).astype(o_ref.dtype)


def _ffn_up(x, gain, w_up, conv_w, conv_b, layer, seq):
    m, d = x.shape
    nj = D_FF // TN_IN
    kern = functools.partial(_ffn_up_kernel, seq // TM_IN)
    return pl.pallas_call(
        kern,
        grid=(m // TM_IN, nj),
        in_specs=[pl.BlockSpec((TM_IN, d), lambda i, j: (i, 0)),
                  pl.BlockSpec((None, 1, d), lambda i, j: (layer, 0, 0)),
                  pl.BlockSpec((None, d, TN_IN), lambda i, j: (layer, 0, j)),
                  pl.BlockSpec((None, d, TN_IN), lambda i, j: (layer, 0, j + nj)),
                  pl.BlockSpec((None, CONV_W, TN_IN), lambda i, j: (layer, 0, j)),
                  pl.BlockSpec((None, CONV_W, TN_IN), lambda i, j: (layer, 0, j + nj)),
                  pl.BlockSpec((None, 1, TN_IN), lambda i, j: (layer, 0, j)),
                  pl.BlockSpec((None, 1, TN_IN), lambda i, j: (layer, 0, j + nj))],
        out_specs=pl.BlockSpec((TM_IN, TN_IN), lambda i, j: (i, j)),
        out_shape=jax.ShapeDtypeStruct((m, D_FF), BF16),
        scratch_shapes=[pltpu.VMEM((TM_IN, d), BF16),
                        pltpu.VMEM((2, nj, SUBLANES, TN_IN), F32)],
        compiler_params=pltpu.CompilerParams(
            dimension_semantics=("arbitrary", "arbitrary"),
            vmem_limit_bytes=VMEM_LIMIT),
        name="ffn_up_conv_gate",
    )(x, gain, w_up, w_up, conv_w, conv_w, conv_b, conv_b)


def _retention_decay_tables():
    lg = jnp.log(1.0 - jnp.power(2.0, -5.0 - jnp.arange(RET_HEADS, dtype=F32)))
    idx = jnp.arange(RET_CHUNK, dtype=F32)
    q_dec = jnp.exp(lg[:, None] * idx)
    k_dec = jnp.exp(-lg[:, None] * idx) * RET_HEAD_DIM ** -0.5
    dec = jnp.concatenate([q_dec, k_dec], axis=0)[..., None]
    dec = jnp.broadcast_to(dec, (2 * RET_HEADS, RET_CHUNK, LANES)).astype(F32)
    chunk_dec = jnp.exp(lg * RET_CHUNK).astype(F32)
    return dec, chunk_dec


def kernel(x, positions, w_in, w_out, w_up, w_down, conv_w, conv_b, attn_sinks,
           pre_mix_norm, post_mix_norm, attn_out_norm, ret_out_norm,
           pre_ffn_norm, post_ffn_norm):
    batch, seq, d = x.shape
    m = batch * seq
    depth = w_in.shape[0]
    xf = x.reshape(m, d)
    pos_b = jnp.broadcast_to(positions.reshape(m, 1).astype(F32), (m, LANES))
    tables = _rope_tables(pos_b)
    dec, cdec = _retention_decay_tables()
    pre_mix = pre_mix_norm[:, None]
    post_mix = post_mix_norm[:, None]
    pre_ffn = pre_ffn_norm[:, None]
    post_ffn = post_ffn_norm[:, None]
    anorm = attn_out_norm[:, None]
    rnorm = ret_out_norm.reshape(depth, 1, D_RET)
    conv_b3 = conv_b[:, None]
    for l in range(depth):
        proj = _in_proj(xf, pre_mix, w_in, l, tables, dec)
        mix = _mixer(proj, attn_sinks[l], cdec, anorm, rnorm, l, batch)
        xf = _matmul_norm_res(mix, w_out, xf, post_mix, l)
        f = _ffn_up(xf, pre_ffn, w_up, conv_w, conv_b3, l, seq)
        xf = _matmul_norm_res(f, w_down, xf, post_ffn, l)
    return xf.reshape(batch, seq, d)
```

```python
import functools

import jax
import jax.numpy as jnp
from jax import lax
from jax.experimental import pallas as pl
from jax.experimental.pallas import tpu as pltpu

F32 = jnp.float32
BF16 = jnp.bfloat16

D_MODEL = 2048
D_ATTN = 1024
HEAD_DIM = 64
N_Q_HEADS = 16
N_KV_HEADS = 4
Q_PER_KV = N_Q_HEADS // N_KV_HEADS
BLOCK = 128
ROPE_DIM = 16
ROPE_THETA = 500000.0
D_RET = 1024
RET_HEADS = 4
RET_HEAD_DIM = 256
RET_THETA = 10000.0
D_KV = N_KV_HEADS * HEAD_DIM
D_IN = D_ATTN + 2 * D_KV + 4 * D_RET
D_FF = 5632
CONV_W = 3
EPS = 1e-6
NEG_INF = -1e30
LOG2E = 1.4426950408889634

LANES = 128
SUBLANES = 8
BF16_ROWS = 16
W_CHUNK_BYTES = 4 * 1024 * 1024
VMEM_LIMIT = 56 * 1024 * 1024
VMEM_BUDGET = 44 * 1024 * 1024

TM_IN = 1024
TN_IN = 512
TM_OUT = 256
NORM_ROWS = 128
MIX_ROWS = 1024
RET_CHUNK = 256
assert MIX_ROWS % RET_CHUNK == 0 and TM_OUT % RET_CHUNK == 0
GELU_C1 = 0.7978845608028654
GELU_C2 = GELU_C1 * 0.044715

Q_TILES = D_ATTN // TN_IN
KV_TILE = Q_TILES
RET_TILE0 = (D_ATTN + 2 * D_KV) // TN_IN
ROPE_R_TILES = (RET_TILE0, RET_TILE0 + 2 * D_RET // TN_IN)
HEADS_PER_TILE = TN_IN // RET_HEAD_DIM
assert 2 * D_KV == TN_IN and (D_ATTN + 2 * D_KV) % TN_IN == 0


def _rms_rows(xf, gain):
    ms = jnp.mean(xf * xf, axis=-1, keepdims=True)
    return xf * lax.rsqrt(ms + EPS) * gain


def _norm_into(x_ref, g_ref, h_ref):
    gain = g_ref[...]

    def body(r, carry):
        rows = pl.ds(pl.multiple_of(r * NORM_ROWS, NORM_ROWS), NORM_ROWS)
        h_ref[rows, :] = _rms_rows(x_ref[rows, :], gain).astype(h_ref.dtype)
        return carry

    lax.fori_loop(0, x_ref.shape[0] // NORM_ROWS, body, 0)


def _rope_table_kernel(pos_ref, inv_r_ref, inv_a_ref, rot_ref, sa_ref, sb_ref,
                       rc_ref, rs_ref, ac_ref, asa_ref, asb_ref):
    pos = pos_ref[...]
    ang_r = pos * inv_r_ref[...]
    rc_ref[...] = jnp.cos(ang_r)
    rs_ref[...] = jnp.sin(ang_r)
    ang_a = pos * inv_a_ref[...]
    ca = jnp.cos(ang_a)
    sa = jnp.sin(ang_a)
    ac_ref[...] = jnp.where(rot_ref[...] > 0.5, ca, 1.0)
    asa_ref[...] = sa * sa_ref[...]
    asb_ref[...] = sa * sb_ref[...]


def _rope_tables(pos_b):
    m = pos_b.shape[0]
    half_r = RET_HEAD_DIM // 2
    inv_r = jnp.power(F32(RET_THETA), -jnp.arange(half_r, dtype=F32) / half_r)
    half_a = ROPE_DIM // 2
    inv_a8 = jnp.power(F32(ROPE_THETA), -jnp.arange(half_a, dtype=F32) / half_a)
    lane = jnp.arange(LANES) % HEAD_DIM
    inv_a = jnp.where(lane < ROPE_DIM, inv_a8[lane % half_a], 0.0).astype(F32)
    rot = (lane < ROPE_DIM).astype(F32)
    sa = jnp.where(lane < half_a, -1.0, 0.0).astype(F32)
    sb = jnp.where((lane >= half_a) & (lane < ROPE_DIM), 1.0, 0.0).astype(F32)
    rows = 2048
    row_spec = pl.BlockSpec((rows, LANES), lambda i: (i, 0))
    lane_spec = pl.BlockSpec((1, LANES), lambda i: (0, 0))
    return pl.pallas_call(
        _rope_table_kernel,
        grid=(m // rows,),
        in_specs=[row_spec] + [lane_spec] * 5,
        out_specs=[row_spec] * 5,
        out_shape=[jax.ShapeDtypeStruct((m, LANES), F32)] * 5,
        compiler_params=pltpu.CompilerParams(
            dimension_semantics=("parallel",), vmem_limit_bytes=VMEM_LIMIT),
        name="rope_tables",
    )(pos_b, inv_r[None], inv_a[None], rot[None], sa[None], sb[None])


def _in_proj_kernel(n_load, x_ref, g_ref, w_ref, ac_ref, asa_ref, asb_ref, rc_ref, rs_ref,
                    dec_ref, o_ref, wb_ref):
    s = pl.program_id(0)

    @pl.when(s < n_load)
    def _():
        _load_weight_chunk(s, w_ref, wb_ref)

    @pl.when(s >= n_load)
    def _():
        _in_proj_tile(x_ref, g_ref, wb_ref, ac_ref, asa_ref, asb_ref, rc_ref, rs_ref, dec_ref, o_ref)


def _in_proj_tile(x_ref, g_ref, w_ref, ac_ref, asa_ref, asb_ref, rc_ref, rs_ref, dec_ref, o_ref):
    tm = x_ref.shape[0]
    h = _rms_rows(x_ref[...], g_ref[...]).astype(BF16)

    def attn_rope(acc, base, n_cols, scale):
        cos, sin_a, sin_b = ac_ref[...], asa_ref[...], asb_ref[...]
        for c in range(n_cols):
            xc = acc[:, c * LANES:(c + 1) * LANES]
            y = (xc * cos + pltpu.roll(xc, LANES - ROPE_DIM // 2, 1) * sin_a
                 + pltpu.roll(xc, ROPE_DIM // 2, 1) * sin_b)
            if scale is not None:
                y = y * scale
            o_ref[:, base + c * LANES:base + (c + 1) * LANES] = y.astype(o_ref.dtype)

    for t in range(w_ref.shape[1] // TN_IN):
        base = t * TN_IN
        acc = jnp.dot(h, w_ref[:, base:base + TN_IN], preferred_element_type=F32)
        if t < Q_TILES:
            attn_rope(acc, base, TN_IN // LANES, HEAD_DIM ** -0.5 * LOG2E)
        elif t == KV_TILE:
            attn_rope(acc, base, D_KV // LANES, None)
            o_ref[:, base + D_KV:base + TN_IN] = acc[:, D_KV:].astype(o_ref.dtype)
        elif t < ROPE_R_TILES[1]:
            half = RET_HEAD_DIM // 2
            cos, sin = rc_ref[...], rs_ref[...]
            for hh in range(HEADS_PER_TILE):
                head = (t - ROPE_R_TILES[0]) * HEADS_PER_TILE + hh
                dec = jnp.concatenate([dec_ref[head]] * (tm // RET_CHUNK), axis=0)
                lo = hh * RET_HEAD_DIM
                x1 = acc[:, lo:lo + half]
                x2 = acc[:, lo + half:lo + 2 * half]
                o_ref[:, base + lo:base + lo + half] = (
                    (x1 * cos - x2 * sin) * dec).astype(o_ref.dtype)
                o_ref[:, base + lo + half:base + lo + 2 * half] = (
                    (x2 * cos + x1 * sin) * dec).astype(o_ref.dtype)
        else:
            o_ref[:, base:base + TN_IN] = acc.astype(o_ref.dtype)


def _in_proj(x, gain, w, layer, tables, dec):
    m, d = x.shape
    n = w.shape[2]
    rc, rs, ac, asa, asb = tables
    chunk = _weight_chunk_rows(d, n)
    n_load = d // chunk

    def tile(s):
        return jnp.maximum(s - n_load, 0), 0

    tab_spec = pl.BlockSpec((TM_OUT, LANES), tile)
    return pl.pallas_call(
        functools.partial(_in_proj_kernel, n_load),
        grid=(n_load + m // TM_OUT,),
        in_specs=[pl.BlockSpec((TM_OUT, d), tile),
                  pl.BlockSpec((None, 1, d), lambda s: (layer, 0, 0)),
                  pl.BlockSpec((None, chunk, n),
                               lambda s: (layer, jnp.minimum(s, n_load - 1), 0)),
                  tab_spec, tab_spec, tab_spec, tab_spec, tab_spec,
                  pl.BlockSpec(dec.shape, lambda s: (0, 0, 0))],
        out_specs=pl.BlockSpec((TM_OUT, n), tile),
        out_shape=jax.ShapeDtypeStruct((m, n), BF16),
        scratch_shapes=[pltpu.VMEM((d, n), BF16)],
        compiler_params=pltpu.CompilerParams(
            dimension_semantics=("arbitrary",), vmem_limit_bytes=VMEM_LIMIT),
        name="norm_in_proj_rope",
    )(x, gain, w, ac, asa, asb, rc, rs, dec)


def _mixer_kernel(qa_ref, kc_ref, kp_ref, vc_ref, vp_ref,
                  qr0_ref, qr1_ref, kr0_ref, kr1_ref, vr0_ref, vr1_ref, gr0_ref, gr1_ref,
                  sink_ref, cdec_ref, anorm_ref, rnorm_ref,
                  o_ref, state_ref, ya_ref):
    n = pl.program_id(1)
    qr_refs, kr_refs = (qr0_ref, qr1_ref), (kr0_ref, kr1_ref)
    vr_refs, gr_refs = (vr0_ref, vr1_ref), (gr0_ref, gr1_ref)

    @pl.when(n == 0)
    def _():
        state_ref[...] = jnp.zeros_like(state_ref)

    lane_b = lax.broadcasted_iota(jnp.int32, (BLOCK, LANES), 1)
    low_b = lane_b < HEAD_DIM
    ones_lo = jnp.where(low_b, 1.0, 0.0).astype(BF16)
    ones_hi = jnp.where(low_b, 0.0, 1.0).astype(BF16)

    def placed(x_b):
        x = x_b.astype(F32)
        xr = pltpu.roll(x, HEAD_DIM, 1)
        views = ((jnp.where(low_b, x, 0.0), jnp.where(low_b, 0.0, xr)),
                 (jnp.where(low_b, xr, 0.0), jnp.where(low_b, 0.0, x)))
        return [tuple(t.astype(BF16) for t in pair) for pair in views]

    def kv_matrices(k_ref, v_ref, rows):
        ks, vs = [], []
        for c in range(D_KV // LANES):
            cols = slice(c * LANES, (c + 1) * LANES)
            for k2, (lo, hi) in zip(placed(k_ref[rows, cols]), placed(v_ref[rows, cols])):
                ks.append(jnp.concatenate(k2, axis=0))
                vs.append(jnp.concatenate([jnp.concatenate([lo, ones_lo], axis=1),
                                           jnp.concatenate([hi, ones_hi], axis=1)], axis=0))
        return ks, vs

    n_sub = qa_ref.shape[0] // BLOCK
    blocks = [kv_matrices(kp_ref, vp_ref, slice(0, BLOCK))]
    blocks += [kv_matrices(kc_ref, vc_ref, slice(sb * BLOCK, (sb + 1) * BLOCK))
               for sb in range(n_sub)]

    qi = lax.broadcasted_iota(jnp.int32, (BLOCK, 2 * BLOCK), 0)
    kj = lax.broadcasted_iota(jnp.int32, (BLOCK, 2 * BLOCK), 1) & (BLOCK - 1)
    in_cur = kj <= qi
    no_prev = jnp.where(n > 0, 0.0, NEG_INF)
    qk_dims = (((1,), (1,)), ((), ()))

    for sb in range(n_sub):
        rows = slice(sb * BLOCK, (sb + 1) * BLOCK)
        (k_prev, v_prev), (k_cur, v_cur) = blocks[sb], blocks[sb + 1]
        for c in range(D_ATTN // LANES):
            g = (2 * c) // Q_PER_KV
            qc = qa_ref[rows, c * LANES:(c + 1) * LANES]
            s_cur = lax.dot_general(qc, k_cur[g], qk_dims, preferred_element_type=F32)
            s_prev = lax.dot_general(qc, k_prev[g], qk_dims, preferred_element_type=F32)
            if sb == 0:
                s_prev = s_prev + no_prev
            s = jnp.where(in_cur, s_cur, s_prev)
            ps, sink_terms = [], []
            for hh in range(2):
                sh = s[:, hh * BLOCK:(hh + 1) * BLOCK]
                sink = sink_ref[2 * c + hh] * LOG2E
                mx = jnp.maximum(jnp.max(sh, axis=-1, keepdims=True), sink)
                ps.append(jnp.exp2(sh - mx))
                sink_terms.append(jnp.exp2(sink - mx))
            p = jnp.concatenate(ps, axis=1)
            acc = (jnp.dot(jnp.where(in_cur, p, 0.0).astype(BF16), v_cur[g],
                           preferred_element_type=F32)
                   + jnp.dot(jnp.where(in_cur, 0.0, p).astype(BF16), v_prev[g],
                             preferred_element_type=F32))
            denom = acc[:, LANES:] + jnp.where(low_b, sink_terms[0], sink_terms[1])
            ya_ref[rows, c * LANES:(c + 1) * LANES] = acc[:, :LANES] / denom

    o_ref[:, :D_ATTN] = _rms_rows(ya_ref[...], anorm_ref[...]).astype(o_ref.dtype)

    ri = lax.broadcasted_iota(jnp.int32, (RET_CHUNK, RET_CHUNK), 0)
    rj = lax.broadcasted_iota(jnp.int32, (RET_CHUNK, RET_CHUNK), 1)
    causal = ri >= rj
    for h in range(RET_HEADS):
        cols = slice(h * RET_HEAD_DIM, (h + 1) * RET_HEAD_DIM)
        pair, side = divmod(h, HEADS_PER_TILE)
        pcols = slice(side * RET_HEAD_DIM, (side + 1) * RET_HEAD_DIM)
        state = state_ref[h]
        for r0 in range(0, qa_ref.shape[0], RET_CHUNK):
            rows = slice(r0, r0 + RET_CHUNK)
            q = qr_refs[pair][rows, pcols]
            k = kr_refs[pair][rows, pcols]
            v = vr_refs[pair][rows, pcols]
            sc = lax.dot_general(q, k, qk_dims, preferred_element_type=F32)
            sc = jnp.where(causal, sc, 0.0).astype(BF16)
            y = (jnp.dot(sc, v, preferred_element_type=F32)
                 + jnp.dot(q, state.astype(BF16), preferred_element_type=F32))
            kv = lax.dot_general(k, v, (((0,), (0,)), ((), ())), preferred_element_type=F32)
            state = (state + kv) * cdec_ref[h]
            y = _rms_rows(y, rnorm_ref[:, cols])
            gate = gr_refs[pair][rows, pcols].astype(F32)
            o_ref[rows, D_ATTN + h * RET_HEAD_DIM:D_ATTN + (h + 1) * RET_HEAD_DIM] = (
                y * (gate * jax.nn.sigmoid(gate))).astype(o_ref.dtype)
        state_ref[h] = state


def _mixer(proj, sinks, cdec, anorm, rnorm, layer, batch):
    m = proj.shape[0]
    steps = m // batch // MIX_ROWS
    sub = MIX_ROWS // BLOCK

    def rows(width, col):
        return pl.BlockSpec((MIX_ROWS, width), lambda b, n: (b * steps + n, col))

    def prev_rows(width, col):
        return pl.BlockSpec(
            (BLOCK, width), lambda b, n: ((b * steps + n) * sub - jnp.minimum(n, 1), col))

    k_col = D_ATTN // D_KV
    ret_specs = [rows(TN_IN, RET_TILE0 + t) for t in range(4 * D_RET // TN_IN)]
    smem = pl.BlockSpec(memory_space=pltpu.SMEM)
    return pl.pallas_call(
        _mixer_kernel,
        grid=(batch, steps),
        in_specs=[rows(D_ATTN, 0),
                  rows(D_KV, k_col), prev_rows(D_KV, k_col),
                  rows(D_KV, k_col + 1), prev_rows(D_KV, k_col + 1),
                  *ret_specs,
                  smem, smem,
                  pl.BlockSpec((None, 1, D_ATTN), lambda b, n: (layer, 0, 0)),
                  pl.BlockSpec((None, 1, D_RET), lambda b, n: (layer, 0, 0))],
        out_specs=rows(D_ATTN + D_RET, 0),
        out_shape=jax.ShapeDtypeStruct((m, D_ATTN + D_RET), BF16),
        scratch_shapes=[pltpu.VMEM((RET_HEADS, RET_HEAD_DIM, RET_HEAD_DIM), F32),
                        pltpu.VMEM((MIX_ROWS, D_ATTN), F32)],
        compiler_params=pltpu.CompilerParams(
            dimension_semantics=("parallel", "arbitrary"),
            vmem_limit_bytes=VMEM_LIMIT),
        name="token_mixer",
    )(*([proj] * (5 + len(ret_specs))), sinks, cdec, anorm, rnorm)


def _weight_chunk_rows(k, n):
    rows = BF16_ROWS
    while rows * 2 * n * 4 <= W_CHUNK_BYTES and k % (rows * 2) == 0:
        rows *= 2
    return rows


def _load_weight_chunk(step, w_ref, wb_ref):
    ck = w_ref.shape[0]
    rows = pl.ds(pl.multiple_of(step * ck, ck), ck)
    wb_ref[rows, :] = w_ref[...].astype(wb_ref.dtype)


def _matmul_norm_res_kernel(n_load, emit_h, a_ref, w_ref, x_ref, g_ref, *refs):
    if emit_h:
        g2_ref, o_ref, h_ref, wb_ref = refs
    else:
        o_ref, wb_ref = refs
    s = pl.program_id(0)

    @pl.when(s < n_load)
    def _():
        _load_weight_chunk(s, w_ref, wb_ref)

    @pl.when(s >= n_load)
    def _():
        for r0 in range(0, a_ref.shape[0], TM_OUT):
            rows = slice(r0, r0 + TM_OUT)
            y = jnp.dot(a_ref[rows, :], wb_ref[...], preferred_element_type=F32)
            x_new = x_ref[rows, :] + _rms_rows(y, g_ref[...])
            o_ref[rows, :] = x_new
            if emit_h:
                h_ref[rows, :] = _rms_rows(x_new, g2_ref[...]).astype(h_ref.dtype)


def _matmul_norm_res(a, w, x, gain, layer, next_gain=None):
    m, k = a.shape
    d = w.shape[2]
    chunk = _weight_chunk_rows(k, d)
    n_load = k // chunk
    resident = k * d * 2 + 2 * chunk * d * 4
    per_row = 2 * (k * 2 + d * 4 + d * 4)
    tm = TM_OUT
    while resident + 2 * tm * per_row <= VMEM_BUDGET and m % (2 * tm) == 0 and tm < TM_IN:
        tm *= 2

    def tile(s):
        return jnp.maximum(s - n_load, 0), 0

    emit_h = next_gain is not None
    gain_spec = pl.BlockSpec((None, 1, d), lambda s: (layer, 0, 0))
    row_spec = pl.BlockSpec((tm, d), tile)
    return pl.pallas_call(
        functools.partial(_matmul_norm_res_kernel, n_load, emit_h),
        grid=(n_load + m // tm,),
        in_specs=[pl.BlockSpec((tm, k), tile),
                  pl.BlockSpec((None, chunk, d),
                               lambda s: (layer, jnp.minimum(s, n_load - 1), 0)),
                  row_spec, gain_spec] + ([gain_spec] if emit_h else []),
        out_specs=[row_spec, row_spec] if emit_h else row_spec,
        out_shape=([jax.ShapeDtypeStruct((m, d), F32), jax.ShapeDtypeStruct((m, d), BF16)]
                   if emit_h else jax.ShapeDtypeStruct((m, d), F32)),
        scratch_shapes=[pltpu.VMEM((k, d), BF16)],
        compiler_params=pltpu.CompilerParams(
            dimension_semantics=("arbitrary",), vmem_limit_bytes=VMEM_LIMIT),
        name="proj_norm_residual",
    )(*((a, w, x, gain, next_gain) if emit_h else (a, w, x, gain)))


def _ffn_up_kernel(tiles_per_seq, h_ref, wa_ref, wg_ref, cwa_ref, cwg_ref,
                   cba_ref, cbg_ref, o_ref, wab_ref, wgb_ref, carry_ref):
    i = pl.program_id(1)
    tm = h_ref.shape[0]

    @pl.when(i == 0)
    def _():
        wab_ref[...] = wa_ref[...].astype(wab_ref.dtype)
        wgb_ref[...] = wg_ref[...].astype(wgb_ref.dtype)
        carry_ref[...] = jnp.zeros_like(carry_ref)

    starts_sequence = (i % tiles_per_seq) == 0
    h = h_ref[...]

    def conv(w_ref, cw, cb, slot):
        u = jnp.dot(h, w_ref[...], preferred_element_type=F32)
        halo = jnp.where(starts_sequence, 0.0, carry_ref[slot])
        carry_ref[slot] = u[tm - SUBLANES:, :]
        ext = jnp.concatenate([halo, u], axis=0)
        d1 = pltpu.roll(ext, 1, 0)[SUBLANES:, :]
        d2 = pltpu.roll(ext, 2, 0)[SUBLANES:, :]
        return cb + d2 * cw[0:1] + d1 * cw[1:2] + u * cw[2:3]

    a = conv(wab_ref, cwa_ref[...], cba_ref[...], 0)
    half_gate = conv(wgb_ref, cwg_ref[...] * 0.5, cbg_ref[...] * 0.5, 1)
    t = jnp.tanh(a * (GELU_C1 + GELU_C2 * (a * a)))
    o_ref[...] = ((a + a * t) * half_gate).astype(o_ref.dtype)


def _ffn_up(h, w_up, conv_w, conv_b, layer, seq):
    m, d = h.shape
    nj = D_FF // TN_IN
    kern = functools.partial(_ffn_up_kernel, seq // TM_IN)
    return pl.pallas_call(
        kern,
        grid=(nj, m // TM_IN),
        in_specs=[pl.BlockSpec((TM_IN, d), lambda j, i: (i, 0)),
                  pl.BlockSpec((None, d, TN_IN), lambda j, i: (layer, 0, j)),
                  pl.BlockSpec((None, d, TN_IN), lambda j, i: (layer, 0, j + nj)),
                  pl.BlockSpec((None, CONV_W, TN_IN), lambda j, i: (layer, 0, j)),
                  pl.BlockSpec((None, CONV_W, TN_IN), lambda j, i: (layer, 0, j + nj)),
                  pl.BlockSpec((None, 1, TN_IN), lambda j, i: (layer, 0, j)),
                  pl.BlockSpec((None, 1, TN_IN), lambda j, i: (layer, 0, j + nj))],
        out_specs=pl.BlockSpec((TM_IN, TN_IN), lambda j, i: (i, j)),
        out_shape=jax.ShapeDtypeStruct((m, D_FF), BF16),
        scratch_shapes=[pltpu.VMEM((d, TN_IN), BF16),
                        pltpu.VMEM((d, TN_IN), BF16),
                        pltpu.VMEM((2, SUBLANES, TN_IN), F32)],
        compiler_params=pltpu.CompilerParams(
            dimension_semantics=("arbitrary", "arbitrary"),
            vmem_limit_bytes=VMEM_LIMIT),
        name="ffn_up_conv_gate",
    )(h, w_up, w_up, conv_w, conv_w, conv_b, conv_b)


def _retention_decay_tables():
    lg = jnp.log(1.0 - jnp.power(2.0, -5.0 - jnp.arange(RET_HEADS, dtype=F32)))
    idx = jnp.arange(RET_CHUNK, dtype=F32)
    q_dec = jnp.exp(lg[:, None] * idx)
    k_dec = jnp.exp(-lg[:, None] * idx) * RET_HEAD_DIM ** -0.5
    dec = jnp.concatenate([q_dec, k_dec], axis=0)[..., None]
    dec = jnp.broadcast_to(dec, (2 * RET_HEADS, RET_CHUNK, LANES)).astype(F32)
    chunk_dec = jnp.exp(lg * RET_CHUNK).astype(F32)
    return dec, chunk_dec


def kernel(x, positions, w_in, w_out, w_up, w_down, conv_w, conv_b, attn_sinks,
           pre_mix_norm, post_mix_norm, attn_out_norm, ret_out_norm,
           pre_ffn_norm, post_ffn_norm):
    batch, seq, d = x.shape
    m = batch * seq
    depth = w_in.shape[0]
    xf = x.reshape(m, d)
    pos_b = jnp.broadcast_to(positions.reshape(m, 1).astype(F32), (m, LANES))
    tables = _rope_tables(pos_b)
    dec, cdec = _retention_decay_tables()
    pre_mix = pre_mix_norm[:, None]
    post_mix = post_mix_norm[:, None]
    pre_ffn = pre_ffn_norm[:, None]
    post_ffn = post_ffn_norm[:, None]
    anorm = attn_out_norm[:, None]
    rnorm = ret_out_norm.reshape(depth, 1, D_RET)
    conv_b3 = conv_b[:, None]
    for l in range(depth):
        proj = _in_proj(xf, pre_mix, w_in, l, tables, dec)
        mix = _mixer(proj, attn_sinks[l], cdec, anorm, rnorm, l, batch)
        xf, h_ffn = _matmul_norm_res(mix, w_out, xf, post_mix, l, next_gain=pre_ffn)
        f = _ffn_up(h_ffn, w_up, conv_w, conv_b3, l, seq)
        xf = _matmul_norm_res(f, w_down, xf, post_ffn, l)
    return xf.reshape(batch, seq, d)
```
